```python
import jax, jax.numpy as jnp
from jax import lax
import numpy as np

D_MODEL = 2048
BATCH = 2
SEQ = 8192
DEPTH = 1
DEC_BATCH = 32
DEC_SEQ = 64
PAST_LEN = 2048

CHUNK = 64
N_META = 16
Q_BLOCK = 128
SB_HEADS = 8
SB_HEAD_DIM = 128
SB_WIDTH = SB_HEADS * SB_HEAD_DIM
RET_HEADS = 8
RET_DK = 128
RET_DV = 256
RET_QK_WIDTH = RET_HEADS * RET_DK
RET_V_WIDTH = RET_HEADS * RET_DV
IN_WIDTH = 3 * SB_WIDTH + 2 * RET_QK_WIDTH + 2 * RET_V_WIDTH + 2 * D_MODEL
N_GROUPS = 4
EXPERTS_PER_GROUP = 4
N_EXPERTS = N_GROUPS * EXPERTS_PER_GROUP
TOP_K_IN_GROUP = 2
D_EXPERT = 512
ROPE_BASE = 10000.0
EPS = 1e-6
F32 = jnp.float32

kernel_name = "stickbreak_retention_hmoe_stream_step"


def _rmsnorm(x, g):
    xf = x.astype(F32)
    xf = xf * lax.rsqrt(jnp.mean(xf * xf, axis=-1, keepdims=True) + EPS)
    return xf.astype(x.dtype) * g


def _rotary(x, pos):
    d = x.shape[-1]
    inv_freq = 1.0 / (ROPE_BASE ** (jnp.arange(0, d, 2, dtype=F32) / d))
    ang = pos[:, None] * inv_freq[None, :]
    cos = jnp.cos(ang)[None, :, None, :]
    sin = jnp.sin(ang)[None, :, None, :]
    xf = x.astype(F32)
    x1, x2 = xf[..., : d // 2], xf[..., d // 2:]
    return jnp.concatenate([x1 * cos - x2 * sin, x1 * sin + x2 * cos], axis=-1).astype(x.dtype)


def _mixer_inputs(u, w_in, pos):
    b, t, _ = u.shape
    sizes = (SB_WIDTH, SB_WIDTH, SB_WIDTH, RET_QK_WIDTH, RET_QK_WIDTH, RET_V_WIDTH, RET_V_WIDTH, D_MODEL, D_MODEL)
    cuts = [int(c) for c in np.cumsum(sizes)[:-1]]
    sb_q, sb_k, sb_v, r_q, r_k, r_v, r_g, g_sb, g_ret = jnp.split(u @ w_in, cuts, axis=-1)
    sb_q = sb_q.reshape(b, t, SB_HEADS, SB_HEAD_DIM)
    sb_k = sb_k.reshape(b, t, SB_HEADS, SB_HEAD_DIM)
    sb_v = sb_v.reshape(b, t, SB_HEADS, SB_HEAD_DIM)
    r_q = _rotary(r_q.reshape(b, t, RET_HEADS, RET_DK), pos)
    r_k = _rotary(r_k.reshape(b, t, RET_HEADS, RET_DK), pos) * (RET_DK ** -0.5)
    r_v = r_v.reshape(b, t, RET_HEADS, RET_DV)
    return sb_q, sb_k, sb_v, r_q, r_k, r_v, r_g, g_sb, g_ret


def _sb_block(q, q_pos, k, v, k_pos):
    z = jnp.einsum("bqhd,bkhd->bhqk", q, k, preferred_element_type=F32) * (SB_HEAD_DIM ** -0.5)
    valid = (k_pos[None, :] < q_pos[:, None])[None, None]
    log_fail = jnp.where(valid, jax.nn.log_sigmoid(-z), 0.0)
    later = lax.cumsum(log_fail, axis=3, reverse=True) - log_fail
    a = jnp.where(valid, jnp.exp(jax.nn.log_sigmoid(z) + later), 0.0)
    return jnp.einsum("bhqk,bkhd->bqhd", a.astype(v.dtype), v)


def _stick_breaking(q, k, v, q_offset):
    n = q.shape[1]
    outs = []
    for start in range(0, n, Q_BLOCK):
        stop = min(start + Q_BLOCK, n)
        kend = q_offset + stop
        q_pos = q_offset + jnp.arange(start, stop)
        outs.append(_sb_block(q[:, start:stop], q_pos, k[:, :kend], v[:, :kend], jnp.arange(kend)))
    return jnp.concatenate(outs, axis=1)


def _ret_log_decay():
    return jnp.log1p(-jnp.power(2.0, -5.0 - jnp.arange(RET_HEADS, dtype=F32)))


def _retention_chunk(q, k, v, s):
    n = q.shape[1]
    log_g = _ret_log_decay()
    t = jnp.arange(n, dtype=F32)
    rel = t[:, None] - t[None, :]
    dmask = jnp.where(rel[None] >= 0, jnp.exp(jnp.maximum(rel, 0.0)[None] * log_g[:, None, None]), 0.0)
    scores = jnp.einsum("bthd,bshd->bhts", q, k) * dmask[None]
    o_inner = jnp.einsum("bhts,bshv->bthv", scores, v)
    o_cross = jnp.einsum("bthd,bhdv->bthv", q, s) * jnp.exp((t + 1.0)[:, None] * log_g[None, :])[None, :, :, None]
    k_dec = k * jnp.exp((n - 1.0 - t)[:, None] * log_g[None, :])[None, :, :, None]
    s_new = jnp.exp(n * log_g)[None, :, None, None] * s + jnp.einsum("bshd,bshv->bhdv", k_dec, v)
    return o_inner + o_cross, s_new


def _retention_from_start(q, k, v):
    b, t, h, dk = q.shape
    pad = (-t) % CHUNK
    nc = (t + pad) // CHUNK

    def to_chunks(a):
        a = jnp.pad(a.astype(F32), ((0, 0), (pad, 0), (0, 0), (0, 0)))
        return jnp.moveaxis(a.reshape(b, nc, CHUNK, h, a.shape[-1]), 1, 0)

    def step(s, qkv):
        o, s = _retention_chunk(qkv[0], qkv[1], qkv[2], s)
        return s, o

    s_fin, o = lax.scan(step, jnp.zeros((b, h, dk, RET_DV), F32), (to_chunks(q), to_chunks(k), to_chunks(v)))
    o = jnp.moveaxis(o, 0, 1).reshape(b, nc * CHUNK, h, RET_DV)[:, pad:]
    return o, s_fin


def _branch_merge(o_sb, o_ret, r_g, g_sb, g_ret, w_sb_o, w_ret_o, w_out):
    b, t = o_sb.shape[:2]
    o_ret = o_ret * lax.rsqrt(jnp.mean(o_ret * o_ret, axis=-1, keepdims=True) + EPS)
    o_ret = o_ret.reshape(b, t, RET_V_WIDTH).astype(r_g.dtype) * jax.nn.silu(r_g)
    sb_branch = o_sb.reshape(b, t, SB_WIDTH) @ w_sb_o
    ret_branch = o_ret @ w_ret_o
    merged = jax.nn.sigmoid(g_sb) * sb_branch + jax.nn.sigmoid(g_ret) * ret_branch
    return merged @ w_out


def _hier_moe(u, w_grp, b_grp, w_exp, b_exp, w_gate, w_up, w_down):
    shp = u.shape
    x = u.reshape(-1, D_MODEL)
    xf = x.astype(F32)
    g_prob = jax.nn.softmax(xf @ w_grp.astype(F32) + b_grp.astype(F32), axis=-1)
    g_top, g_idx = lax.top_k(g_prob, 1)
    e_logits = (xf @ w_exp.astype(F32) + b_exp.astype(F32)).reshape(-1, N_GROUPS, EXPERTS_PER_GROUP)
    e_in_group = jnp.take_along_axis(e_logits, g_idx[:, :, None], axis=1)[:, 0]
    e_top, e_idx = lax.top_k(e_in_group, TOP_K_IN_GROUP)
    e_w = jax.nn.softmax(e_top, axis=-1) * g_top
    expert_id = g_idx * EXPERTS_PER_GROUP + e_idx
    combine = jnp.sum(jax.nn.one_hot(expert_id, N_EXPERTS, dtype=F32) * e_w[..., None], axis=1).astype(x.dtype)
    y = jnp.zeros_like(x)
    for e in range(N_EXPERTS):
        hid = jax.nn.silu(x @ w_gate[e]) * (x @ w_up[e])
        y = y + combine[:, e:e + 1] * (hid @ w_down[e])
    return y.reshape(shp)


def _layer(h, pos, past_k, past_v, past_state, norm1_g, w_in, w_sb_o, w_ret_o, w_out, norm2_g,
           w_grp, b_grp, w_exp, b_exp, w_gate, w_up, w_down):
    u = _rmsnorm(h, norm1_g)
    sb_q, sb_k, sb_v, r_q, r_k, r_v, r_g, g_sb, g_ret = _mixer_inputs(u, w_in, pos)
    if past_k is None:
        o_sb = _stick_breaking(sb_q, sb_k, sb_v, 0)
        o_ret, new_state = _retention_from_start(r_q, r_k, r_v)
    else:
        past = past_k.shape[1]
        k_all = jnp.concatenate([past_k.astype(sb_k.dtype), sb_k], axis=1)
        v_all = jnp.concatenate([past_v.astype(sb_v.dtype), sb_v], axis=1)
        o_sb = _stick_breaking(sb_q, k_all, v_all, past)
        o_ret, new_state = _retention_chunk(r_q.astype(F32), r_k.astype(F32), r_v.astype(F32), past_state.astype(F32))
    h = h + _branch_merge(o_sb, o_ret, r_g, g_sb, g_ret, w_sb_o, w_ret_o, w_out)
    h = h + _hier_moe(_rmsnorm(h, norm2_g), w_grp, b_grp, w_exp, b_exp, w_gate, w_up, w_down)
    return h, sb_k, sb_v, new_state.astype(h.dtype)


def setup_inputs(seed: int = 0) -> dict:
    key = jax.random.key(seed)
    ks = jax.random.split(key, 24)

    def nrm(k, shape, scale):
        return jax.random.normal(k, shape, F32) * scale

    return {
        "x_prompt": nrm(ks[0], (BATCH, SEQ, D_MODEL), 1.0),
        "x_sample": nrm(ks[1], (DEC_BATCH, DEC_SEQ, D_MODEL), 1.0),
        "cache_sb_k": nrm(ks[2], (DEPTH, DEC_BATCH, PAST_LEN, SB_HEADS, SB_HEAD_DIM), 1.0),
        "cache_sb_v": nrm(ks[3], (DEPTH, DEC_BATCH, PAST_LEN, SB_HEADS, SB_HEAD_DIM), 1.0),
        "state_ret": nrm(ks[4], (DEPTH, DEC_BATCH, RET_HEADS, RET_DK, RET_DV), 1.0),
        "meta": nrm(ks[5], (N_META, D_MODEL), 1.0),
        "norm1_g": 1.0 + nrm(ks[6], (DEPTH, D_MODEL), 0.02),
        "w_in": nrm(ks[7], (DEPTH, D_MODEL, IN_WIDTH), D_MODEL ** -0.5),
        "w_sb_o": nrm(ks[8], (DEPTH, SB_WIDTH, D_MODEL), SB_WIDTH ** -0.5),
        "w_ret_o": nrm(ks[9], (DEPTH, RET_V_WIDTH, D_MODEL), RET_V_WIDTH ** -0.5),
        "w_out": nrm(ks[10], (DEPTH, D_MODEL, D_MODEL), D_MODEL ** -0.5),
        "norm2_g": 1.0 + nrm(ks[11], (DEPTH, D_MODEL), 0.02),
        "w_grp": nrm(ks[12], (DEPTH, D_MODEL, N_GROUPS), D_MODEL ** -0.5),
        "b_grp": nrm(ks[13], (DEPTH, N_GROUPS), 0.01),
        "w_exp": nrm(ks[14], (DEPTH, D_MODEL, N_EXPERTS), D_MODEL ** -0.5),
        "b_exp": nrm(ks[15], (DEPTH, N_EXPERTS), 0.01),
        "w_gate": nrm(ks[16], (DEPTH, N_EXPERTS, D_MODEL, D_EXPERT), D_MODEL ** -0.5),
        "w_up": nrm(ks[17], (DEPTH, N_EXPERTS, D_MODEL, D_EXPERT), D_MODEL ** -0.5),
        "w_down": nrm(ks[18], (DEPTH, N_EXPERTS, D_EXPERT, D_MODEL), D_EXPERT ** -0.5),
        "normf_g": 1.0 + nrm(ks[19], (D_MODEL,), 0.02),
    }


def reference(x_prompt, x_sample, cache_sb_k, cache_sb_v, state_ret, meta, norm1_g, w_in, w_sb_o, w_ret_o,
              w_out, norm2_g, w_grp, b_grp, w_exp, b_exp, w_gate, w_up, w_down, normf_g):
    b = x_prompt.shape[0]
    h_p = jnp.concatenate([jnp.broadcast_to(meta[None].astype(x_prompt.dtype), (b, N_META, D_MODEL)), x_prompt], axis=1)
    pos_p = jnp.arange(h_p.shape[1], dtype=F32) - N_META
    past = cache_sb_k.shape[2]
    h_s = x_sample
    pos_s = past + jnp.arange(x_sample.shape[1], dtype=F32)
    kp, vp, sp, ksm, vsm, ssm = [], [], [], [], [], []
    for i in range(DEPTH):
        lw = (norm1_g[i], w_in[i], w_sb_o[i], w_ret_o[i], w_out[i], norm2_g[i],
              w_grp[i], b_grp[i], w_exp[i], b_exp[i], w_gate[i], w_up[i], w_down[i])
        h_p, k_new, v_new, s_new = _layer(h_p, pos_p, None, None, None, *lw)
        kp.append(k_new)
        vp.append(v_new)
        sp.append(s_new)
        h_s, k_new, v_new, s_new = _layer(h_s, pos_s, cache_sb_k[i], cache_sb_v[i], state_ret[i], *lw)
        ksm.append(k_new)
        vsm.append(v_new)
        ssm.append(s_new)
    y_prompt = _rmsnorm(h_p, normf_g)[:, N_META:]
    y_sample = _rmsnorm(h_s, normf_g)
    return (y_prompt, y_sample, jnp.stack(kp, 0), jnp.stack(vp, 0), jnp.stack(sp, 0),
            jnp.stack(ksm, 0), jnp.stack(vsm, 0), jnp.stack(ssm, 0))
```

```python
import functools

import jax
import jax.numpy as jnp
from jax import lax
from jax.experimental import pallas as pl
from jax.experimental.pallas import tpu as pltpu

F32 = jnp.float32
BF16 = jnp.bfloat16

N_META = 16
HEADS = 8
SB_DH = 128
RET_DK = 128
RET_DV = 256
N_GROUPS = 4
EXPERTS_PER_GROUP = 4
N_PAIRS = 6
N_CLASSES = N_GROUPS * N_PAIRS
ROPE_BASE = 10000.0
EPS = 1e-6
SB_LOG_CUTOFF = -88.0

ROW_TILE = 512
COL_TILE = 1024
MERGE_TILE = 256
MOE_TILE = 256
SB_TILE = 128
RET_CHUNK = 256
VMEM_LIMIT = 56 * 1024 * 1024


def _cparams(*sem):
    return pltpu.CompilerParams(dimension_semantics=sem, vmem_limit_bytes=VMEM_LIMIT)


def _rms(x, g):
    return (x * lax.rsqrt(jnp.mean(x * x, axis=-1, keepdims=True) + EPS)) * g


def _dot(a, b):
    return jnp.dot(a, b, preferred_element_type=F32)


def _dot_nt(a, b):
    return lax.dot_general(a, b, (((1,), (1,)), ((), ())), preferred_element_type=F32)


def _dot_tn(a, b):
    return lax.dot_general(a, b, (((0,), (0,)), ((), ())), preferred_element_type=F32)


def _norm_kernel(xp_ref, xs_ref, meta_ref, g_ref, u_ref, *, n_p, n_s):
    i = pl.program_id(0)

    @pl.when(i < n_p)
    def _():
        u_ref[...] = _rms(xp_ref[...], g_ref[...]).astype(BF16)

    @pl.when((i >= n_p) & (i < n_p + n_s))
    def _():
        u_ref[...] = _rms(xs_ref[...], g_ref[...]).astype(BF16)

    @pl.when(i >= n_p + n_s)
    def _():
        u_ref[...] = jnp.zeros_like(u_ref)
        u_ref[0:N_META, :] = _rms(meta_ref[...], g_ref[...]).astype(BF16)


def _norm_all(xp, xs, meta, g, tm):
    d = xp.shape[1]
    n_p, n_s = xp.shape[0] // tm, xs.shape[0] // tm
    return pl.pallas_call(
        functools.partial(_norm_kernel, n_p=n_p, n_s=n_s),
        grid=(n_p + n_s + 1,),
        in_specs=[
            pl.BlockSpec((tm, d), lambda i: (jnp.minimum(i, n_p - 1), 0)),
            pl.BlockSpec((tm, d), lambda i: (jnp.clip(i - n_p, 0, n_s - 1), 0)),
            pl.BlockSpec((N_META, d), lambda i: (0, 0)),
            pl.BlockSpec((1, d), lambda i: (0, 0)),
        ],
        out_specs=pl.BlockSpec((tm, d), lambda i: (i, 0)),
        out_shape=jax.ShapeDtypeStruct(((n_p + n_s + 1) * tm, d), BF16),
        compiler_params=_cparams("arbitrary"),
        name="norm1",
    )(xp, xs, meta, g)


def _proj_kernel(u_ref, w_ref, *refs, mode, scale):
    acc = _dot(u_ref[...], w_ref[...])
    if mode == "f32":
        refs[0][...] = acc
    elif mode == "bf16":
        refs[0][...] = acc.astype(BF16)
    elif mode == "both":
        refs[0][...] = acc
        refs[1][...] = acc.astype(BF16)
    else:
        cos_ref, sin_ref, o_ref = refs
        cos, sin = cos_ref[...], sin_ref[...]
        for hh in range(acc.shape[1] // RET_DK):
            x = acc[:, hh * RET_DK:(hh + 1) * RET_DK]
            r = x * cos + pltpu.roll(x, RET_DK // 2, 1) * sin
            if scale is not None:
                r = r * scale
            o_ref[:, hh * RET_DK:(hh + 1) * RET_DK] = r.astype(o_ref.dtype)


def _project(u, w, col0, width, mode, tm, tn, rot=None, scale=None, out_dtype=None):
    m, d = u.shape
    tn = min(tn, width)
    n_j, n_i = width // tn, m // tm
    j0 = col0 // tn
    in_specs = [
        pl.BlockSpec((tm, d), lambda j, i: (i, 0)),
        pl.BlockSpec((d, tn), lambda j, i: (0, j0 + j)),
    ]
    args = [u, w]
    ospec = pl.BlockSpec((tm, tn), lambda j, i: (i, j))
    if mode == "rot":
        in_specs += [pl.BlockSpec((tm, RET_DK), lambda j, i: (i, 0))] * 2
        args += list(rot)
        out_specs, out_shape = ospec, jax.ShapeDtypeStruct((m, width), out_dtype)
    elif mode == "both":
        out_specs = [ospec, ospec]
        out_shape = [jax.ShapeDtypeStruct((m, width), F32), jax.ShapeDtypeStruct((m, width), BF16)]
    else:
        out_specs = ospec
        out_shape = jax.ShapeDtypeStruct((m, width), F32 if mode == "f32" else BF16)
    return pl.pallas_call(
        functools.partial(_proj_kernel, mode=mode, scale=scale),
        grid=(n_j, n_i),
        in_specs=in_specs,
        out_specs=out_specs,
        out_shape=out_shape,
        compiler_params=_cparams("arbitrary", "arbitrary"),
        name="proj_" + mode,
    )(*args)


def _upper_ones(n):
    r = lax.broadcasted_iota(jnp.int32, (n, n), 0)
    c = lax.broadcasted_iota(jnp.int32, (n, n), 1)
    return (r > c).astype(BF16)


def _sb_tile(q, k, v, carry, acc, tri, causal):
    z = _dot_nt(q, k) * (SB_DH ** -0.5)
    sp = jnp.maximum(z, 0.0) + jnp.log(1.0 + jnp.exp(-jnp.abs(z)))
    if causal:
        r = lax.broadcasted_iota(jnp.int32, z.shape, 0)
        c = lax.broadcasted_iota(jnp.int32, z.shape, 1)
        valid = c < r
        lf = jnp.where(valid, -sp, 0.0)
    else:
        lf = -sp
    hi = lf.astype(BF16)
    lo = (lf - hi.astype(F32)).astype(BF16)
    later = _dot(hi, tri) + _dot(lo, tri)
    a = jnp.exp((z - sp) + (later + carry))
    if causal:
        a = jnp.where(valid, a, 0.0)
    acc = acc + _dot(a.astype(BF16), v)
    carry = carry + jnp.sum(lf, axis=-1, keepdims=True)
    return carry, acc


def _sb_prompt_kernel(q_ref, k_ref, v_ref, mk_ref, mv_ref, o_ref, *, t, n_q):
    tri = _upper_ones(t)
    tri_m = _upper_ones(N_META)

    def q_tile(i, _):
        r0 = pl.multiple_of(i * t, t)
        q = q_ref[pl.ds(r0, t), :]
        carry = jnp.zeros((t, 1), F32)
        acc = jnp.zeros((t, SB_DH), F32)
        carry, acc = _sb_tile(q, k_ref[pl.ds(r0, t), :], v_ref[pl.ds(r0, t), :], carry, acc, tri, True)

        def cond(s):
            return (s[0] >= 0) & (jnp.max(s[1]) > SB_LOG_CUTOFF)

        def body(s):
            j, carry, acc = s
            c0 = pl.multiple_of(j * t, t)
            carry, acc = _sb_tile(q, k_ref[pl.ds(c0, t), :], v_ref[pl.ds(c0, t), :], carry, acc, tri, False)
            return j - 1, carry, acc

        _, carry, acc = lax.while_loop(cond, body, (i - 1, carry, acc))
        acc = lax.cond(
            jnp.max(carry) > SB_LOG_CUTOFF,
            lambda: _sb_tile(q, mk_ref[...], mv_ref[...], carry, acc, tri_m, False)[1],
            lambda: acc,
        )
        o_ref[pl.ds(r0, t), :] = acc.astype(o_ref.dtype)
        return 0

    lax.fori_loop(0, n_q, q_tile, 0)


def _sb_prompt(q16, k16, v16, batch, seq, meta_row0, t):
    n_q = seq // t
    mb = meta_row0 // N_META
    blk = lambda b, h: (b, h)
    mblk = lambda b, h: (mb, h)
    return pl.pallas_call(
        functools.partial(_sb_prompt_kernel, t=t, n_q=n_q),
        grid=(batch, HEADS),
        in_specs=[
            pl.BlockSpec((seq, SB_DH), blk),
            pl.BlockSpec((seq, SB_DH), blk),
            pl.BlockSpec((seq, SB_DH), blk),
            pl.BlockSpec((N_META, SB_DH), mblk),
            pl.BlockSpec((N_META, SB_DH), mblk),
        ],
        out_specs=pl.BlockSpec((seq, SB_DH), blk),
        out_shape=jax.ShapeDtypeStruct((batch * seq, HEADS * SB_DH), BF16),
        compiler_params=_cparams("arbitrary", "arbitrary"),
        name="sb_prompt",
    )(q16, k16, v16, k16, v16)


def _sb_sample_kernel(q_ref, k_ref, v_ref, ck_ref, cv_ref, o_ref, *, n, t, n_past):
    q = q_ref[...]
    carry = jnp.zeros((n, 1), F32)
    acc = jnp.zeros((n, SB_DH), F32)
    carry, acc = _sb_tile(q, k_ref[...], v_ref[...], carry, acc, _upper_ones(n), True)
    tri = _upper_ones(t)

    def cond(s):
        return (s[0] >= 0) & (jnp.max(s[1]) > SB_LOG_CUTOFF)

    def body(s):
        j, carry, acc = s
        c0 = pl.multiple_of(j * t, t)
        kt = ck_ref[0, pl.ds(c0, t), :].astype(BF16)
        vt = cv_ref[0, pl.ds(c0, t), :].astype(BF16)
        carry, acc = _sb_tile(q, kt, vt, carry, acc, tri, False)
        return j - 1, carry, acc

    _, carry, acc = lax.while_loop(cond, body, (n_past - 1, carry, acc))
    o_ref[...] = acc.astype(o_ref.dtype)


def _sb_sample(q16, k16, v16, cache_k, cache_v, dec_b, n, row0, t):
    past = cache_k.shape[1]
    t = min(t, past)
    rb = row0 // n
    blk = lambda b, h: (rb + b, h)
    cblk = lambda b, h: (b, 0, h)
    return pl.pallas_call(
        functools.partial(_sb_sample_kernel, n=n, t=t, n_past=past // t),
        grid=(dec_b, HEADS),
        in_specs=[
            pl.BlockSpec((n, SB_DH), blk),
            pl.BlockSpec((n, SB_DH), blk),
            pl.BlockSpec((n, SB_DH), blk),
            pl.BlockSpec((1, past, SB_DH), cblk),
            pl.BlockSpec((1, past, SB_DH), cblk),
        ],
        out_specs=pl.BlockSpec((n, SB_DH), lambda b, h: (b, h)),
        out_shape=jax.ShapeDtypeStruct((dec_b * n, HEADS * SB_DH), BF16),
        compiler_params=_cparams("arbitrary", "arbitrary"),
        name="sb_sample",
    )(q16, k16, v16, cache_k, cache_v)


def _ret_kernel(lg_ref, q_ref, k_ref, v_ref, g_ref, s0_ref, o_ref, s_out_ref, s_scr, d_scr, *, c, n_c):
    h = pl.program_id(1)
    ci = pl.program_id(2)
    lg = lg_ref[h]

    @pl.when(ci == 0)
    def _():
        s_scr[...] = s0_ref[0, 0]
        rel = (lax.broadcasted_iota(jnp.int32, (c, c), 0) - lax.broadcasted_iota(jnp.int32, (c, c), 1)).astype(F32)
        d_scr[...] = jnp.where(rel >= 0, jnp.exp(jnp.maximum(rel, 0.0) * lg), 0.0)

    q = q_ref[...]
    kf = k_ref[...]
    v = v_ref[...]
    s = s_scr[...]
    tcol = lax.broadcasted_iota(jnp.int32, (c, 1), 0).astype(F32)
    scores = _dot_nt(q, kf.astype(BF16)) * d_scr[...]
    o = _dot(scores.astype(BF16), v) + _dot(q, s.astype(BF16)) * jnp.exp((tcol + 1.0) * lg)
    k_dec = (kf * jnp.exp((c - 1.0 - tcol) * lg)).astype(BF16)
    s_new = jnp.exp(c * lg) * s + _dot_tn(k_dec, v)
    s_scr[...] = s_new

    o = o * lax.rsqrt(jnp.mean(o * o, axis=-1, keepdims=True) + EPS)
    g = g_ref[...]
    o_ref[...] = (o * (g * jax.nn.sigmoid(g))).astype(o_ref.dtype)

    @pl.when(ci == n_c - 1)
    def _():
        s_out_ref[0, 0] = s_new


def _retention(log_g, rq16, rk32, rv16, gates, s0, batch, rows_per_batch, row0, c, shared_s0):
    n_c = rows_per_batch // c
    rb = row0 // c
    row = lambda b, h, ci: (rb + b * n_c + ci, h)
    sidx = (lambda b, h, ci: (0, h, 0, 0)) if shared_s0 else (lambda b, h, ci: (b, h, 0, 0))
    return pl.pallas_call(
        functools.partial(_ret_kernel, c=c, n_c=n_c),
        grid=(batch, HEADS, n_c),
        in_specs=[
            pl.BlockSpec(memory_space=pltpu.SMEM),
            pl.BlockSpec((c, RET_DK), row),
            pl.BlockSpec((c, RET_DK), row),
            pl.BlockSpec((c, RET_DV), row),
            pl.BlockSpec((c, RET_DV), row),
            pl.BlockSpec((1, 1, RET_DK, RET_DV), sidx),
        ],
        out_specs=[
            pl.BlockSpec((c, RET_DV), lambda b, h, ci: (b * n_c + ci, h)),
            pl.BlockSpec((1, 1, RET_DK, RET_DV), lambda b, h, ci: (b, h, 0, 0)),
        ],
        out_shape=[
            jax.ShapeDtypeStruct((batch * rows_per_batch, HEADS * RET_DV), BF16),
            jax.ShapeDtypeStruct((batch, HEADS, RET_DK, RET_DV), F32),
        ],
        scratch_shapes=[pltpu.VMEM((RET_DK, RET_DV), F32), pltpu.VMEM((c, c), F32)],
        compiler_params=_cparams("arbitrary", "arbitrary", "arbitrary"),
        name="retention",
    )(log_g, rq16, rk32, rv16, gates, s0)


def _route(logits):
    gl = [logits[r:r + 1, :] for r in range(N_GROUPS)]
    gmax = functools.reduce(jnp.maximum, gl)
    g_idx = jnp.where(gl[0] == gmax, 0, jnp.where(gl[1] == gmax, 1, jnp.where(gl[2] == gmax, 2, 3)))
    g_top = 1.0 / functools.reduce(lambda a, b: a + b, [jnp.exp(x - gmax) for x in gl])
    e = []
    for r in range(EXPERTS_PER_GROUP):
        rows = [logits[N_GROUPS + g * EXPERTS_PER_GROUP + r:N_GROUPS + g * EXPERTS_PER_GROUP + r + 1, :]
                for g in range(N_GROUPS)]
        e.append(jnp.where(g_idx == 0, rows[0], jnp.where(g_idx == 1, rows[1], jnp.where(g_idx == 2, rows[2], rows[3]))))
    m1 = functools.reduce(jnp.maximum, e)
    i1 = jnp.where(e[0] == m1, 0, jnp.where(e[1] == m1, 1, jnp.where(e[2] == m1, 2, 3)))
    e2 = [jnp.where(i1 == r, -jnp.inf, e[r]) for r in range(EXPERTS_PER_GROUP)]
    m2 = functools.reduce(jnp.maximum, e2)
    i2 = jnp.where(e2[0] == m2, 0, jnp.where(e2[1] == m2, 1, jnp.where(e2[2] == m2, 2, 3)))
    p2 = jnp.exp(m2 - m1)
    w1 = g_top / (1.0 + p2)
    w2 = g_top * p2 / (1.0 + p2)
    lo = jnp.minimum(i1, i2)
    hi = jnp.maximum(i1, i2)
    w_lo = jnp.where(i1 < i2, w1, w2)
    w_hi = jnp.where(i1 < i2, w2, w1)
    pair = jnp.where(lo == 0, hi - 1, jnp.where(lo == 1, hi + 1, 5))
    return g_idx * N_PAIRS + pair, w_lo, w_hi


def _merge_kernel(osp_ref, oss_ref, orp_ref, ors_ref, gsb_ref, gret_ref, xp_ref, xs_ref,
                  wsb_ref, wret_ref, wout_ref, n2g_ref, wr_ref, br_ref, h_ref, route_ref, *, n_p):
    is_p = pl.program_id(0) < n_p
    o_sb = jnp.where(is_p, osp_ref[...], oss_ref[...])
    o_ret = jnp.where(is_p, orp_ref[...], ors_ref[...])
    x = jnp.where(is_p, xp_ref[...], xs_ref[...])
    merged = (jax.nn.sigmoid(gsb_ref[...]) * _dot(o_sb, wsb_ref[...])
              + jax.nn.sigmoid(gret_ref[...]) * _dot(o_ret, wret_ref[...]))
    h = x + _dot(merged.astype(BF16), wout_ref[...])
    h_ref[...] = h
    u2 = _rms(h, n2g_ref[...])
    logits = lax.dot_general(wr_ref[...], u2, (((1,), (1,)), ((), ())),
                             preferred_element_type=F32, precision=lax.Precision.HIGHEST) + br_ref[...]
    cls, w_lo, w_hi = _route(logits)
    route_ref[...] = jnp.zeros_like(route_ref)
    route_ref[0:1, :] = cls.astype(F32)
    route_ref[1:2, :] = w_lo
    route_ref[2:3, :] = w_hi


def _merge(osp, oss, orp, ors, gates, xp, xs, wsb, wret, wout, n2g, wr, br, tm):
    d = xp.shape[1]
    n_p, n_s = xp.shape[0] // tm, xs.shape[0] // tm
    n = n_p + n_s
    pidx = lambda i: (jnp.minimum(i, n_p - 1), 0)
    sidx = lambda i: (jnp.clip(i - n_p, 0, n_s - 1), 0)
    gb = RET_DV * HEADS // d
    const = lambda i: (0, 0)
    res = functools.partial(pl.BlockSpec, pipeline_mode=pl.Buffered(1))
    return pl.pallas_call(
        functools.partial(_merge_kernel, n_p=n_p),
        grid=(n,),
        in_specs=[
            pl.BlockSpec((tm, HEADS * SB_DH), pidx),
            pl.BlockSpec((tm, HEADS * SB_DH), sidx),
            pl.BlockSpec((tm, HEADS * RET_DV), pidx),
            pl.BlockSpec((tm, HEADS * RET_DV), sidx),
            pl.BlockSpec((tm, d), lambda i: (i, gb)),
            pl.BlockSpec((tm, d), lambda i: (i, gb + 1)),
            pl.BlockSpec((tm, d), pidx),
            pl.BlockSpec((tm, d), sidx),
            res(wsb.shape, const),
            res(wret.shape, const),
            res(wout.shape, const),
            pl.BlockSpec((1, d), const),
            res(wr.shape, const),
            pl.BlockSpec(br.shape, const),
        ],
        out_specs=[
            pl.BlockSpec((tm, d), lambda i: (i, 0)),
            pl.BlockSpec((8, tm), lambda i: (0, i)),
        ],
        out_shape=[
            jax.ShapeDtypeStruct((n * tm, d), F32),
            jax.ShapeDtypeStruct((8, n * tm), F32),
        ],
        compiler_params=_cparams("arbitrary"),
        name="merge",
    )(osp, oss, orp, ors, gates, gates, xp, xs, wsb, wret, wout, n2g, wr, br)


def _moe_kernel(elo_ref, ehi_ref, nvalid_ref, src_ref, wts_ref, h_hbm, n2g_ref, nfg_ref,
                wg_lo, wu_lo, wd_lo, wg_hi, wu_hi, wd_hi, y_hbm, hbuf, ybuf, sem_in, sem_out, *, tm):
    t = pl.program_id(0)
    n_valid = nvalid_ref[t]

    @pl.when(n_valid > 0)
    def _():
        def gather(r, c):
            pltpu.make_async_copy(h_hbm.at[pl.ds(src_ref[0, 0, r], 1), :], hbuf.at[pl.ds(r, 1), :], sem_in).start()
            return c

        lax.fori_loop(0, tm, gather, 0)

        def gather_wait(r, c):
            pltpu.make_async_copy(h_hbm.at[pl.ds(0, 1), :], hbuf.at[pl.ds(r, 1), :], sem_in).wait()
            return c

        lax.fori_loop(0, tm, gather_wait, 0)

        hrows = hbuf[...]
        u = _rms(hrows, n2g_ref[...]).astype(BF16)
        y = jnp.zeros_like(hrows)
        for col, (wg, wu, wd) in enumerate(((wg_lo, wu_lo, wd_lo), (wg_hi, wu_hi, wd_hi))):
            gate = _dot(u, wg[0])
            hid = (gate * jax.nn.sigmoid(gate)) * _dot(u, wu[0])
            y = y + wts_ref[:, col:col + 1] * _dot(hid.astype(BF16), wd[0])
        ybuf[...] = _rms(hrows + y, nfg_ref[...])

        def scatter(r, c):
            pltpu.make_async_copy(ybuf.at[pl.ds(r, 1), :], y_hbm.at[pl.ds(src_ref[0, 0, r], 1), :], sem_out).start()
            return c

        lax.fori_loop(0, n_valid, scatter, 0)

        def scatter_wait(r, c):
            pltpu.make_async_copy(ybuf.at[pl.ds(r, 1), :], y_hbm.at[pl.ds(0, 1), :], sem_out).wait()
            return c

        lax.fori_loop(0, n_valid, scatter_wait, 0)


def _moe(h, elo, ehi, nvalid, src, wts, n2g, nfg, wg, wu, wd, tm):
    n, d = h.shape
    f = wg.shape[-1]
    n_t = src.shape[0]
    lo = lambda t, elo, ehi, nu: (elo[t], 0, 0)
    hi = lambda t, elo, ehi, nu: (ehi[t], 0, 0)
    const = lambda t, elo, ehi, nu: (0, 0)
    grid_spec = pltpu.PrefetchScalarGridSpec(
        num_scalar_prefetch=3,
        grid=(n_t,),
        in_specs=[
            pl.BlockSpec((1, 1, tm), lambda t, *_: (t, 0, 0), memory_space=pltpu.SMEM),
            pl.BlockSpec((tm, 2), lambda t, *_: (t, 0)),
            pl.BlockSpec(memory_space=pl.ANY),
            pl.BlockSpec((1, d), const),
            pl.BlockSpec((1, d), const),
            pl.BlockSpec((1, d, f), lo),
            pl.BlockSpec((1, d, f), lo),
            pl.BlockSpec((1, f, d), lo),
            pl.BlockSpec((1, d, f), hi),
            pl.BlockSpec((1, d, f), hi),
            pl.BlockSpec((1, f, d), hi),
        ],
        out_specs=pl.BlockSpec(memory_space=pl.ANY),
        scratch_shapes=[
            pltpu.VMEM((tm, d), F32),
            pltpu.VMEM((tm, d), F32),
            pltpu.SemaphoreType.DMA(()),
            pltpu.SemaphoreType.DMA(()),
        ],
    )
    return pl.pallas_call(
        functools.partial(_moe_kernel, tm=tm),
        grid_spec=grid_spec,
        out_shape=jax.ShapeDtypeStruct((n, d), F32),
        compiler_params=_cparams("arbitrary"),
        name="moe",
    )(elo, ehi, nvalid, src, wts, h, n2g, nfg, wg, wu, wd, wg, wu, wd)


def _sort_by_class(route, tm):
    n = route.shape[1]
    n_t = n // tm + N_CLASSES
    cls = route[0].astype(jnp.int32)
    onehot = (cls[:, None] == jnp.arange(N_CLASSES, dtype=jnp.int32)[None, :]).astype(jnp.int32)
    rank = jnp.sum((jnp.cumsum(onehot, axis=0) - onehot) * onehot, axis=1)
    tiles_per = (jnp.sum(onehot, axis=0) + tm - 1) // tm
    tile_end = jnp.cumsum(tiles_per)
    nused = tile_end[-1]
    pos = (tile_end - tiles_per)[cls] * tm + rank
    tile_ids = jnp.minimum(jnp.arange(n_t, dtype=jnp.int32), nused - 1)
    tile_cls = jnp.minimum(jnp.searchsorted(tile_end, tile_ids, side="right"), N_CLASSES - 1).astype(jnp.int32)
    pair_lo = jnp.array([0, 0, 0, 1, 1, 2], jnp.int32)
    pair_hi = jnp.array([1, 2, 3, 2, 3, 3], jnp.int32)
    elo = (tile_cls // N_PAIRS) * EXPERTS_PER_GROUP + pair_lo[tile_cls % N_PAIRS]
    ehi = (tile_cls // N_PAIRS) * EXPERTS_PER_GROUP + pair_hi[tile_cls % N_PAIRS]
    p = n_t * tm
    src = jnp.full((p,), -1, jnp.int32).at[pos].set(jnp.arange(n, dtype=jnp.int32))
    nvalid = jnp.sum((src >= 0).reshape(n_t, tm), axis=1).astype(jnp.int32)
    src = jnp.maximum(src, 0)
    wts = jnp.zeros((p, 2), F32).at[pos].set(route[1:3].T)
    return elo, ehi, nvalid, src.reshape(n_t, 1, tm), wts


def kernel(x_prompt, x_sample, cache_sb_k, cache_sb_v, state_ret, meta, norm1_g, w_in, w_sb_o, w_ret_o,
           w_out, norm2_g, w_grp, b_grp, w_exp, b_exp, w_gate, w_up, w_down, normf_g):
    batch, seq, d = x_prompt.shape
    dec_b, dec_n, _ = x_sample.shape
    depth, _, past = cache_sb_k.shape[:3]
    assert depth == 1
    n_p, n_s = batch * seq, dec_b * dec_n
    sbw, qkw, vw = HEADS * SB_DH, HEADS * RET_DK, HEADS * RET_DV
    tm = min(ROW_TILE, n_s)

    xp = x_prompt.reshape(n_p, d)
    xs = x_sample.reshape(n_s, d)
    u = _norm_all(xp, xs, meta, norm1_g, tm)
    m_rows = u.shape[0]
    meta_row0 = n_p + n_s

    pos = jnp.concatenate([
        jnp.tile(jnp.arange(seq, dtype=F32), batch),
        jnp.tile(past + jnp.arange(dec_n, dtype=F32), dec_b),
        jnp.arange(m_rows - meta_row0, dtype=F32) - N_META,
    ])
    inv_freq = 1.0 / (ROPE_BASE ** (jnp.arange(0, RET_DK, 2, dtype=F32) / RET_DK))
    ang = pos[:, None] * inv_freq[None, :]
    cos_t = jnp.concatenate([jnp.cos(ang), jnp.cos(ang)], axis=1)
    sin_t = jnp.concatenate([-jnp.sin(ang), jnp.sin(ang)], axis=1)

    w_in16 = w_in[0].astype(BF16)
    proj = functools.partial(_project, u, w_in16, tm=tm, tn=COL_TILE)
    c = 0
    sb_q16 = proj(c, sbw, "bf16"); c += sbw
    sb_k32, sb_k16 = proj(c, sbw, "both"); c += sbw
    sb_v32, sb_v16 = proj(c, sbw, "both"); c += sbw
    r_q16 = proj(c, qkw, "rot", rot=(cos_t, sin_t), out_dtype=BF16); c += qkw
    r_k32 = proj(c, qkw, "rot", rot=(cos_t, sin_t), scale=RET_DK ** -0.5, out_dtype=F32); c += qkw
    r_v16 = proj(c, vw, "bf16"); c += vw
    gates = proj(c, vw + 2 * d, "f32")

    o_sb_p = _sb_prompt(sb_q16, sb_k16, sb_v16, batch, seq, meta_row0, min(SB_TILE, seq))
    ck = cache_sb_k[0].reshape(dec_b, past, sbw)
    cv = cache_sb_v[0].reshape(dec_b, past, sbw)
    o_sb_s = _sb_sample(sb_q16, sb_k16, sb_v16, ck, cv, dec_b, dec_n, n_p, SB_TILE)

    log_g = jnp.log1p(-jnp.power(2.0, -5.0 - jnp.arange(HEADS, dtype=F32)))
    ret = functools.partial(_retention, log_g, r_q16, r_k32, r_v16, gates)
    zero_state = jnp.zeros((1, HEADS, RET_DK, RET_DV), F32)
    _, s_meta = ret(zero_state, 1, N_META, meta_row0, N_META, True)
    o_ret_p, s_p = ret(s_meta, batch, seq, 0, min(RET_CHUNK, seq), True)
    o_ret_s, s_s = ret(state_ret[0], dec_b, dec_n, n_p, dec_n, False)

    w_router = jnp.zeros((32, d), F32).at[:N_GROUPS].set(w_grp[0].T).at[N_GROUPS:N_GROUPS + 16].set(w_exp[0].T)
    b_router = jnp.zeros((32, 1), F32).at[:N_GROUPS, 0].set(b_grp[0]).at[N_GROUPS:N_GROUPS + 16, 0].set(b_exp[0])
    tmm = min(MERGE_TILE, n_s)
    h, route = _merge(o_sb_p, o_sb_s, o_ret_p, o_ret_s, gates, xp, xs,
                      w_sb_o[0].astype(BF16), w_ret_o[0].astype(BF16), w_out[0].astype(BF16),
                      norm2_g, w_router, b_router, tmm)

    tmo = min(MOE_TILE, n_s)
    elo, ehi, nvalid, src, wts = _sort_by_class(route, tmo)
    y = _moe(h, elo, ehi, nvalid, src, wts, norm2_g, normf_g.reshape(1, d),
             w_gate[0].astype(BF16), w_up[0].astype(BF16), w_down[0].astype(BF16), tmo)

    y_prompt = y[:n_p].reshape(batch, seq, d)
    y_sample = y[n_p:n_p + n_s].reshape(dec_b, dec_n, d)

    def with_meta(a32):
        m = jnp.broadcast_to(a32[meta_row0:meta_row0 + N_META][None], (batch, N_META, sbw))
        full = jnp.concatenate([m, a32[:n_p].reshape(batch, seq, sbw)], axis=1)
        return full.reshape(1, batch, N_META + seq, HEADS, SB_DH)

    def sample_kv(a32):
        return a32[n_p:n_p + n_s].reshape(1, dec_b, dec_n, HEADS, SB_DH)

    return (y_prompt, y_sample, with_meta(sb_k32), with_meta(sb_v32), s_p[None],
            sample_kv(sb_k32), sample_kv(sb_v32), s_s[None])
```

```python
import functools

import jax
import jax.numpy as jnp
from jax import lax
from jax.experimental import pallas as pl
from jax.experimental.pallas import tpu as pltpu

F32 = jnp.float32
BF16 = jnp.bfloat16

N_META = 16
HEADS = 8
SB_DH = 128
RET_DK = 128
RET_DV = 256
N_GROUPS = 4
EXPERTS_PER_GROUP = 4
N_PAIRS = 6
N_CLASSES = N_GROUPS * N_PAIRS
ROPE_BASE = 10000.0
EPS = 1e-6
SB_LOG_CUTOFF = -88.0

ROW_TILE = 512
COL_TILE = 1024
MERGE_TILE = 512
MOE_TILE = 256
SB_TILE = 128
RET_CHUNK = 256
VMEM_LIMIT = 56 * 1024 * 1024


def _cparams(*sem):
    return pltpu.CompilerParams(dimension_semantics=sem, vmem_limit_bytes=VMEM_LIMIT)


def _rms(x, g):
    return (x * lax.rsqrt(jnp.mean(x * x, axis=-1, keepdims=True) + EPS)) * g


def _dot(a, b):
    return jnp.dot(a, b, preferred_element_type=F32)


def _dot_nt(a, b):
    return lax.dot_general(a, b, (((1,), (1,)), ((), ())), preferred_element_type=F32)


def _dot_tn(a, b):
    return lax.dot_general(a, b, (((0,), (0,)), ((), ())), preferred_element_type=F32)


def _norm_kernel(xp_ref, xs_ref, meta_ref, g_ref, u_ref, *, n_p, n_s):
    i = pl.program_id(0)

    @pl.when(i < n_p)
    def _():
        u_ref[...] = _rms(xp_ref[...], g_ref[...]).astype(BF16)

    @pl.when((i >= n_p) & (i < n_p + n_s))
    def _():
        u_ref[...] = _rms(xs_ref[...], g_ref[...]).astype(BF16)

    @pl.when(i >= n_p + n_s)
    def _():
        u_ref[...] = jnp.zeros_like(u_ref)
        u_ref[0:N_META, :] = _rms(meta_ref[...], g_ref[...]).astype(BF16)


def _norm_all(xp, xs, meta, g, tm):
    d = xp.shape[1]
    n_p, n_s = xp.shape[0] // tm, xs.shape[0] // tm
    return pl.pallas_call(
        functools.partial(_norm_kernel, n_p=n_p, n_s=n_s),
        grid=(n_p + n_s + 1,),
        in_specs=[
            pl.BlockSpec((tm, d), lambda i: (jnp.minimum(i, n_p - 1), 0)),
            pl.BlockSpec((tm, d), lambda i: (jnp.clip(i - n_p, 0, n_s - 1), 0)),
            pl.BlockSpec((N_META, d), lambda i: (0, 0)),
            pl.BlockSpec((1, d), lambda i: (0, 0)),
        ],
        out_specs=pl.BlockSpec((tm, d), lambda i: (i, 0)),
        out_shape=jax.ShapeDtypeStruct(((n_p + n_s + 1) * tm, d), BF16),
        compiler_params=_cparams("arbitrary"),
        name="norm1",
    )(xp, xs, meta, g)


def _proj_kernel(u_ref, w_ref, *refs, mode, scale):
    acc = _dot(u_ref[...], w_ref[...])
    if mode == "f32":
        refs[0][...] = acc
    elif mode == "bf16":
        refs[0][...] = acc.astype(BF16)
    elif mode == "both":
        refs[0][...] = acc
        refs[1][...] = acc.astype(BF16)
    else:
        cos_ref, sin_ref, o_ref = refs
        cos, sin = cos_ref[...], sin_ref[...]
        for hh in range(acc.shape[1] // RET_DK):
            x = acc[:, hh * RET_DK:(hh + 1) * RET_DK]
            r = x * cos + pltpu.roll(x, RET_DK // 2, 1) * sin
            if scale is not None:
                r = r * scale
            o_ref[:, hh * RET_DK:(hh + 1) * RET_DK] = r.astype(o_ref.dtype)


def _project(u, w, col0, width, mode, tm, tn, rot=None, scale=None, out_dtype=None):
    m, d = u.shape
    tn = min(tn, width)
    n_j, n_i = width // tn, m // tm
    j0 = col0 // tn
    in_specs = [
        pl.BlockSpec((tm, d), lambda j, i: (i, 0)),
        pl.BlockSpec((d, tn), lambda j, i: (0, j0 + j)),
    ]
    args = [u, w]
    ospec = pl.BlockSpec((tm, tn), lambda j, i: (i, j))
    if mode == "rot":
        in_specs += [pl.BlockSpec((tm, RET_DK), lambda j, i: (i, 0))] * 2
        args += list(rot)
        out_specs, out_shape = ospec, jax.ShapeDtypeStruct((m, width), out_dtype)
    elif mode == "both":
        out_specs = [ospec, ospec]
        out_shape = [jax.ShapeDtypeStruct((m, width), F32), jax.ShapeDtypeStruct((m, width), BF16)]
    else:
        out_specs = ospec
        out_shape = jax.ShapeDtypeStruct((m, width), F32 if mode == "f32" else BF16)
    return pl.pallas_call(
        functools.partial(_proj_kernel, mode=mode, scale=scale),
        grid=(n_j, n_i),
        in_specs=in_specs,
        out_specs=out_specs,
        out_shape=out_shape,
        compiler_params=_cparams("arbitrary", "arbitrary"),
        name="proj_" + mode,
    )(*args)


def _upper_ones(n):
    r = lax.broadcasted_iota(jnp.int32, (n, n), 0)
    c = lax.broadcasted_iota(jnp.int32, (n, n), 1)
    return (r > c).astype(BF16)


def _sb_tile(q, k, v, carry, acc, tri, valid=None):
    z = _dot_nt(q, k) * (SB_DH ** -0.5)
    sp = jnp.maximum(z, 0.0) + jnp.log(1.0 + jnp.exp(-jnp.abs(z)))
    lf = -sp if valid is None else jnp.where(valid, -sp, 0.0)
    hi = lf.astype(BF16)
    lo = (lf - hi.astype(F32)).astype(BF16)
    later = _dot(hi, tri) + _dot(lo, tri)
    a = jnp.exp((z - sp) + (later + carry))
    if valid is not None:
        a = jnp.where(valid, a, 0.0)
    acc = acc + _dot(a.astype(BF16), v)
    carry = carry + jnp.sum(lf, axis=-1, keepdims=True)
    return carry, acc


def _sb_prompt_kernel(q_ref, k_ref, v_ref, mk_ref, mv_ref, o_ref, *, t, n_q, unroll):
    tri = _upper_ones(t)
    tri2 = _upper_ones(2 * t)
    tri_m = _upper_ones(N_META)
    col_minus_row = (lax.broadcasted_iota(jnp.int32, (t, 2 * t), 1)
                     - lax.broadcasted_iota(jnp.int32, (t, 2 * t), 0))

    def window(i):
        r0 = pl.multiple_of(i * t, t)
        w0 = pl.multiple_of(jnp.maximum(i - 1, 0) * t, t)
        q = q_ref[pl.ds(r0, t), :]
        valid = col_minus_row < (r0 - w0)
        carry, acc = _sb_tile(q, k_ref[pl.ds(w0, 2 * t), :], v_ref[pl.ds(w0, 2 * t), :],
                              jnp.zeros((t, 1), F32), jnp.zeros((t, SB_DH), F32), tri2, valid)
        return q, carry, acc

    def older_keys(i, q, carry, acc):
        def cond(s):
            return (s[0] >= 0) & (jnp.max(s[1]) > SB_LOG_CUTOFF)

        def body(s):
            j, carry, acc = s
            c0 = pl.multiple_of(j * t, t)
            carry, acc = _sb_tile(q, k_ref[pl.ds(c0, t), :], v_ref[pl.ds(c0, t), :], carry, acc, tri)
            return j - 1, carry, acc

        _, carry, acc = lax.while_loop(cond, body, (i - 2, carry, acc))
        return lax.cond(
            jnp.max(carry) > SB_LOG_CUTOFF,
            lambda: _sb_tile(q, mk_ref[...], mv_ref[...], carry, acc, tri_m)[1],
            lambda: acc,
        )

    def group(p, _):
        tiles = [window(p * unroll + u) for u in range(unroll)]
        for u, (q, carry, acc) in enumerate(tiles):
            i = p * unroll + u
            acc = lax.cond(
                jnp.max(carry) > SB_LOG_CUTOFF,
                functools.partial(older_keys, i, q, carry, acc),
                lambda acc=acc: acc,
            )
            o_ref[pl.ds(pl.multiple_of(i * t, t), t), :] = acc.astype(o_ref.dtype)
        return 0

    lax.fori_loop(0, n_q // unroll, group, 0)


def _sb_prompt(q16, k16, v16, batch, seq, meta_row0, t):
    n_q = seq // t
    assert n_q >= 2
    unroll = 2 if n_q % 2 == 0 else 1
    mb = meta_row0 // N_META
    blk = lambda b, h: (b, h)
    mblk = lambda b, h: (mb, h)
    return pl.pallas_call(
        functools.partial(_sb_prompt_kernel, t=t, n_q=n_q, unroll=unroll),
        grid=(batch, HEADS),
        in_specs=[
            pl.BlockSpec((seq, SB_DH), blk),
            pl.BlockSpec((seq, SB_DH), blk),
            pl.BlockSpec((seq, SB_DH), blk),
            pl.BlockSpec((N_META, SB_DH), mblk),
            pl.BlockSpec((N_META, SB_DH), mblk),
        ],
        out_specs=pl.BlockSpec((seq, SB_DH), blk),
        out_shape=jax.ShapeDtypeStruct((batch * seq, HEADS * SB_DH), BF16),
        compiler_params=_cparams("arbitrary", "arbitrary"),
        name="sb_prompt",
    )(q16, k16, v16, k16, v16)


def _sb_sample_kernel(q_ref, k_ref, v_ref, ckl_ref, cvl_ref, ck_hbm, cv_hbm, o_ref, kbuf, vbuf, sem,
                      *, n, t, n_past):
    b = pl.program_id(0)
    tri_n = _upper_ones(n)
    tri = _upper_ones(t)
    causal = lax.broadcasted_iota(jnp.int32, (n, n), 1) < lax.broadcasted_iota(jnp.int32, (n, n), 0)

    def older_keys(hd, q, carry, acc):
        def cond(s):
            return (s[0] >= 0) & (jnp.max(s[1]) > SB_LOG_CUTOFF)

        def body(s):
            j, carry, acc = s
            rows = pl.ds(pl.multiple_of(j * t, t), t)
            ck = pltpu.make_async_copy(ck_hbm.at[0, b, rows], kbuf, sem.at[0])
            cv = pltpu.make_async_copy(cv_hbm.at[0, b, rows], vbuf, sem.at[1])
            ck.start()
            cv.start()
            ck.wait()
            cv.wait()
            carry, acc = _sb_tile(q, kbuf[:, hd, :].astype(BF16), vbuf[:, hd, :].astype(BF16), carry, acc, tri)
            return j - 1, carry, acc

        return lax.while_loop(cond, body, (n_past - 2, carry, acc))[2]

    heads = []
    for hd in range(HEADS):
        cols = slice(hd * SB_DH, (hd + 1) * SB_DH)
        q = q_ref[:, cols]
        carry, acc = _sb_tile(q, k_ref[:, cols], v_ref[:, cols], jnp.zeros((n, 1), F32),
                              jnp.zeros((n, SB_DH), F32), tri_n, causal)
        carry, acc = _sb_tile(q, ckl_ref[0, 0, :, hd, :].astype(BF16), cvl_ref[0, 0, :, hd, :].astype(BF16),
                              carry, acc, tri)
        heads.append((q, carry, acc))
    for hd, (q, carry, acc) in enumerate(heads):
        acc = lax.cond(
            jnp.max(carry) > SB_LOG_CUTOFF,
            functools.partial(older_keys, hd, q, carry, acc),
            lambda acc=acc: acc,
        )
        o_ref[:, hd * SB_DH:(hd + 1) * SB_DH] = acc.astype(o_ref.dtype)


def _sb_sample(q16, k16, v16, cache_k, cache_v, n, row0, t):
    _, dec_b, past = cache_k.shape[:3]
    t = min(t, past)
    n_past = past // t
    rb = row0 // n
    w = HEADS * SB_DH
    blk = lambda b: (rb + b, 0)
    last = lambda b: (0, b, n_past - 1, 0, 0)
    return pl.pallas_call(
        functools.partial(_sb_sample_kernel, n=n, t=t, n_past=n_past),
        grid=(dec_b,),
        in_specs=[
            pl.BlockSpec((n, w), blk),
            pl.BlockSpec((n, w), blk),
            pl.BlockSpec((n, w), blk),
            pl.BlockSpec((1, 1, t, HEADS, SB_DH), last),
            pl.BlockSpec((1, 1, t, HEADS, SB_DH), last),
            pl.BlockSpec(memory_space=pl.ANY),
            pl.BlockSpec(memory_space=pl.ANY),
        ],
        out_specs=pl.BlockSpec((n, w), lambda b: (b, 0)),
        out_shape=jax.ShapeDtypeStruct((dec_b * n, w), BF16),
        scratch_shapes=[
            pltpu.VMEM((t, HEADS, SB_DH), F32),
            pltpu.VMEM((t, HEADS, SB_DH), F32),
            pltpu.SemaphoreType.DMA((2,)),
        ],
        compiler_params=_cparams("arbitrary"),
        name="sb_sample",
    )(q16, k16, v16, cache_k, cache_v, cache_k, cache_v)


def _ret_kernel(lg_ref, q_ref, k_ref, v_ref, g_ref, s0_ref, o_ref, s_out_ref, s_scr, d_scr, *, c, n_c):
    ci = pl.program_id(1)

    @pl.when((pl.program_id(0) == 0) & (ci == 0))
    def _():
        rel = (lax.broadcasted_iota(jnp.int32, (c, c), 0) - lax.broadcasted_iota(jnp.int32, (c, c), 1)).astype(F32)
        for hd in range(HEADS):
            d_scr[hd] = jnp.where(rel >= 0, jnp.exp(jnp.maximum(rel, 0.0) * lg_ref[hd]), 0.0)

    @pl.when(ci == 0)
    def _():
        s_scr[...] = s0_ref[0]

    tcol = lax.broadcasted_iota(jnp.int32, (c, 1), 0).astype(F32)
    for hd in range(HEADS):
        lg = lg_ref[hd]
        qk = slice(hd * RET_DK, (hd + 1) * RET_DK)
        vv = slice(hd * RET_DV, (hd + 1) * RET_DV)
        q = q_ref[:, qk]
        kf = k_ref[:, qk]
        v = v_ref[:, vv]
        s = s_scr[hd]
        scores = _dot_nt(q, kf.astype(BF16)) * d_scr[hd]
        o = _dot(scores.astype(BF16), v) + _dot(q, s.astype(BF16)) * jnp.exp((tcol + 1.0) * lg)
        k_dec = (kf * jnp.exp((c - 1.0 - tcol) * lg)).astype(BF16)
        s_scr[hd] = jnp.exp(c * lg) * s + _dot_tn(k_dec, v)
        o = o * lax.rsqrt(jnp.mean(o * o, axis=-1, keepdims=True) + EPS)
        g = g_ref[:, vv]
        o_ref[:, vv] = (o * (g * jax.nn.sigmoid(g))).astype(o_ref.dtype)

    @pl.when(ci == n_c - 1)
    def _():
        s_out_ref[0] = s_scr[...]


def _retention(log_g, rq16, rk32, rv16, gates, s0, batch, rows_per_batch, row0, c, shared_s0):
    n_c = rows_per_batch // c
    rb = row0 // c
    qkw, vw = HEADS * RET_DK, HEADS * RET_DV
    row = lambda b, ci: (rb + b * n_c + ci, 0)
    sidx = (lambda b, ci: (0, 0, 0, 0)) if shared_s0 else (lambda b, ci: (b, 0, 0, 0))
    return pl.pallas_call(
        functools.partial(_ret_kernel, c=c, n_c=n_c),
        grid=(batch, n_c),
        in_specs=[
            pl.BlockSpec(memory_space=pltpu.SMEM),
            pl.BlockSpec((c, qkw), row),
            pl.BlockSpec((c, qkw), row),
            pl.BlockSpec((c, vw), row),
            pl.BlockSpec((c, vw), row),
            pl.BlockSpec((1, HEADS, RET_DK, RET_DV), sidx),
        ],
        out_specs=[
            pl.BlockSpec((c, vw), lambda b, ci: (b * n_c + ci, 0)),
            pl.BlockSpec((1, HEADS, RET_DK, RET_DV), lambda b, ci: (b, 0, 0, 0)),
        ],
        out_shape=[
            jax.ShapeDtypeStruct((batch * rows_per_batch, vw), BF16),
            jax.ShapeDtypeStruct((batch, HEADS, RET_DK, RET_DV), F32),
        ],
        scratch_shapes=[pltpu.VMEM((HEADS, RET_DK, RET_DV), F32), pltpu.VMEM((HEADS, c, c), F32)],
        compiler_params=_cparams("arbitrary", "arbitrary"),
        name="retention",
    )(log_g, rq16, rk32, rv16, gates, s0)


def _route(logits):
    gl = [logits[r:r + 1, :] for r in range(N_GROUPS)]
    gmax = functools.reduce(jnp.maximum, gl)
    g_idx = jnp.where(gl[0] == gmax, 0, jnp.where(gl[1] == gmax, 1, jnp.where(gl[2] == gmax, 2, 3)))
    g_top = 1.0 / functools.reduce(lambda a, b: a + b, [jnp.exp(x - gmax) for x in gl])
    e = []
    for r in range(EXPERTS_PER_GROUP):
        rows = [logits[N_GROUPS + g * EXPERTS_PER_GROUP + r:N_GROUPS + g * EXPERTS_PER_GROUP + r + 1, :]
                for g in range(N_GROUPS)]
        e.append(jnp.where(g_idx == 0, rows[0], jnp.where(g_idx == 1, rows[1], jnp.where(g_idx == 2, rows[2], rows[3]))))
    m1 = functools.reduce(jnp.maximum, e)
    i1 = jnp.where(e[0] == m1, 0, jnp.where(e[1] == m1, 1, jnp.where(e[2] == m1, 2, 3)))
    e2 = [jnp.where(i1 == r, -jnp.inf, e[r]) for r in range(EXPERTS_PER_GROUP)]
    m2 = functools.reduce(jnp.maximum, e2)
    i2 = jnp.where(e2[0] == m2, 0, jnp.where(e2[1] == m2, 1, jnp.where(e2[2] == m2, 2, 3)))
    p2 = jnp.exp(m2 - m1)
    w1 = g_top / (1.0 + p2)
    w2 = g_top * p2 / (1.0 + p2)
    lo = jnp.minimum(i1, i2)
    hi = jnp.maximum(i1, i2)
    w_lo = jnp.where(i1 < i2, w1, w2)
    w_hi = jnp.where(i1 < i2, w2, w1)
    pair = jnp.where(lo == 0, hi - 1, jnp.where(lo == 1, hi + 1, 5))
    return g_idx * N_PAIRS + pair, w_lo, w_hi


def _gate_merge_kernel(osp_ref, oss_ref, orp_ref, ors_ref, gsb_ref, gret_ref, wsb_ref, wret_ref, m_ref, *, n_p):
    is_p = pl.program_id(1) < n_p
    o_sb = jnp.where(is_p, osp_ref[...], oss_ref[...])
    o_ret = jnp.where(is_p, orp_ref[...], ors_ref[...])
    merged = (jax.nn.sigmoid(gsb_ref[...]) * _dot(o_sb, wsb_ref[...])
              + jax.nn.sigmoid(gret_ref[...]) * _dot(o_ret, wret_ref[...]))
    m_ref[...] = merged.astype(m_ref.dtype)


def _gate_merge(osp, oss, orp, ors, gates, wsb, wret, tm, tn):
    d = wsb.shape[1]
    tn = min(tn, d)
    n_p, n_s = osp.shape[0] // tm, oss.shape[0] // tm
    n_j = d // tn
    vw = HEADS * RET_DV
    pidx = lambda j, i: (jnp.minimum(i, n_p - 1), 0)
    sidx = lambda j, i: (jnp.clip(i - n_p, 0, n_s - 1), 0)
    return pl.pallas_call(
        functools.partial(_gate_merge_kernel, n_p=n_p),
        grid=(n_j, n_p + n_s),
        in_specs=[
            pl.BlockSpec((tm, HEADS * SB_DH), pidx),
            pl.BlockSpec((tm, HEADS * SB_DH), sidx),
            pl.BlockSpec((tm, vw), pidx),
            pl.BlockSpec((tm, vw), sidx),
            pl.BlockSpec((tm, tn), lambda j, i: (i, vw // tn + j)),
            pl.BlockSpec((tm, tn), lambda j, i: (i, (vw + d) // tn + j)),
            pl.BlockSpec((wsb.shape[0], tn), lambda j, i: (0, j)),
            pl.BlockSpec((wret.shape[0], tn), lambda j, i: (0, j)),
        ],
        out_specs=pl.BlockSpec((tm, tn), lambda j, i: (i, j)),
        out_shape=jax.ShapeDtypeStruct(((n_p + n_s) * tm, d), BF16),
        compiler_params=_cparams("arbitrary", "arbitrary"),
        name="gate_merge",
    )(osp, oss, orp, ors, gates, gates, wsb, wret)


def _out_kernel(m_ref, xp_ref, xs_ref, wout_ref, n2g_ref, wrh_ref, wrl_ref, br_ref, h_ref, route_ref, *, n_p):
    x = jnp.where(pl.program_id(0) < n_p, xp_ref[...], xs_ref[...])
    h = x + _dot(m_ref[...], wout_ref[...])
    h_ref[...] = h
    u2 = _rms(h, n2g_ref[...])
    u_hi = u2.astype(BF16)
    u_lo = (u2 - u_hi.astype(F32)).astype(BF16)
    logits = (_dot(u_hi, wrh_ref[...]) + (_dot(u_hi, wrl_ref[...]) + _dot(u_lo, wrh_ref[...]))) + br_ref[...]
    cls, w_lo, w_hi = _route(logits.T)
    route_ref[...] = jnp.zeros_like(route_ref)
    route_ref[0:1, :] = cls.astype(F32)
    route_ref[1:2, :] = w_lo
    route_ref[2:3, :] = w_hi


def _out_proj(m, xp, xs, wout, n2g, wr_hi, wr_lo, br, tm):
    d = xp.shape[1]
    n_p, n_s = xp.shape[0] // tm, xs.shape[0] // tm
    n = n_p + n_s
    const = lambda i: (0, 0)
    return pl.pallas_call(
        functools.partial(_out_kernel, n_p=n_p),
        grid=(n,),
        in_specs=[
            pl.BlockSpec((tm, d), lambda i: (i, 0)),
            pl.BlockSpec((tm, d), lambda i: (jnp.minimum(i, n_p - 1), 0)),
            pl.BlockSpec((tm, d), lambda i: (jnp.clip(i - n_p, 0, n_s - 1), 0)),
            pl.BlockSpec(wout.shape, const, pipeline_mode=pl.Buffered(1)),
            pl.BlockSpec((1, d), const),
            pl.BlockSpec(wr_hi.shape, const),
            pl.BlockSpec(wr_lo.shape, const),
            pl.BlockSpec(br.shape, const),
        ],
        out_specs=[
            pl.BlockSpec((tm, d), lambda i: (i, 0)),
            pl.BlockSpec((8, tm), lambda i: (0, i)),
        ],
        out_shape=[
            jax.ShapeDtypeStruct((n * tm, d), F32),
            jax.ShapeDtypeStruct((8, n * tm), F32),
        ],
        compiler_params=_cparams("arbitrary"),
        name="out_proj",
    )(m, xp, xs, wout, n2g, wr_hi, wr_lo, br)


def _moe_kernel(elo_ref, ehi_ref, nvalid_ref, src_ref, wts_ref, h_hbm, n2g_ref, nfg_ref,
                wg_lo, wu_lo, wd_lo, wg_hi, wu_hi, wd_hi, yp_hbm, ys_hbm, hbuf, ybuf, sem_in, sem_out,
                *, tm, n_p):
    t = pl.program_id(0)
    n_valid = nvalid_ref[t]

    @pl.when(n_valid > 0)
    def _():
        def gather(r, c):
            pltpu.make_async_copy(h_hbm.at[pl.ds(src_ref[0, 0, r], 1), :], hbuf.at[pl.ds(r, 1), :], sem_in).start()
            return c

        lax.fori_loop(0, tm, gather, 0)

        def gather_wait(r, c):
            pltpu.make_async_copy(h_hbm.at[pl.ds(0, 1), :], hbuf.at[pl.ds(r, 1), :], sem_in).wait()
            return c

        lax.fori_loop(0, tm, gather_wait, 0)

        hrows = hbuf[...]
        u = _rms(hrows, n2g_ref[...]).astype(BF16)
        y = jnp.zeros_like(hrows)
        for col, (wg, wu, wd) in enumerate(((wg_lo, wu_lo, wd_lo), (wg_hi, wu_hi, wd_hi))):
            gate = _dot(u, wg[0])
            hid = (gate * jax.nn.sigmoid(gate)) * _dot(u, wu[0])
            y = y + wts_ref[:, col:col + 1] * _dot(hid.astype(BF16), wd[0])
        ybuf[...] = _rms(hrows + y, nfg_ref[...])

        def scatter(r, c):
            row = src_ref[0, 0, r]

            @pl.when(row < n_p)
            def _():
                pltpu.make_async_copy(ybuf.at[pl.ds(r, 1), :], yp_hbm.at[pl.ds(row, 1), :], sem_out).start()

            @pl.when(row >= n_p)
            def _():
                pltpu.make_async_copy(ybuf.at[pl.ds(r, 1), :], ys_hbm.at[pl.ds(row - n_p, 1), :], sem_out).start()

            return c

        lax.fori_loop(0, n_valid, scatter, 0)

        def scatter_wait(r, c):
            pltpu.make_async_copy(ybuf.at[pl.ds(r, 1), :], yp_hbm.at[pl.ds(0, 1), :], sem_out).wait()
            return c

        lax.fori_loop(0, n_valid, scatter_wait, 0)


def _moe(h, elo, ehi, nvalid, src, wts, n2g, nfg, wg, wu, wd, tm, n_p):
    n, d = h.shape
    f = wg.shape[-1]
    n_t = src.shape[0]
    lo = lambda t, elo, ehi, nu: (elo[t], 0, 0)
    hi = lambda t, elo, ehi, nu: (ehi[t], 0, 0)
    const = lambda t, elo, ehi, nu: (0, 0)
    grid_spec = pltpu.PrefetchScalarGridSpec(
        num_scalar_prefetch=3,
        grid=(n_t,),
        in_specs=[
            pl.BlockSpec((1, 1, tm), lambda t, *_: (t, 0, 0), memory_space=pltpu.SMEM),
            pl.BlockSpec((tm, 2), lambda t, *_: (t, 0)),
            pl.BlockSpec(memory_space=pl.ANY),
            pl.BlockSpec((1, d), const),
            pl.BlockSpec((1, d), const),
            pl.BlockSpec((1, d, f), lo),
            pl.BlockSpec((1, d, f), lo),
            pl.BlockSpec((1, f, d), lo),
            pl.BlockSpec((1, d, f), hi),
            pl.BlockSpec((1, d, f), hi),
            pl.BlockSpec((1, f, d), hi),
        ],
        out_specs=[pl.BlockSpec(memory_space=pl.ANY), pl.BlockSpec(memory_space=pl.ANY)],
        scratch_shapes=[
            pltpu.VMEM((tm, d), F32),
            pltpu.VMEM((tm, d), F32),
            pltpu.SemaphoreType.DMA(()),
            pltpu.SemaphoreType.DMA(()),
        ],
    )
    return pl.pallas_call(
        functools.partial(_moe_kernel, tm=tm, n_p=n_p),
        grid_spec=grid_spec,
        out_shape=[jax.ShapeDtypeStruct((n_p, d), F32), jax.ShapeDtypeStruct((n - n_p, d), F32)],
        compiler_params=_cparams("arbitrary"),
        name="moe",
    )(elo, ehi, nvalid, src, wts, h, n2g, nfg, wg, wu, wd, wg, wu, wd)


def _sort_by_class(route, tm):
    n = route.shape[1]
    n_t = n // tm + N_CLASSES
    cls = route[0].astype(jnp.int32)
    onehot = (cls[:, None] == jnp.arange(N_CLASSES, dtype=jnp.int32)[None, :]).astype(jnp.int32)
    rank = jnp.sum((jnp.cumsum(onehot, axis=0) - onehot) * onehot, axis=1)
    tiles_per = (jnp.sum(onehot, axis=0) + tm - 1) // tm
    tile_end = jnp.cumsum(tiles_per)
    nused = tile_end[-1]
    pos = (tile_end - tiles_per)[cls] * tm + rank
    tile_ids = jnp.minimum(jnp.arange(n_t, dtype=jnp.int32), nused - 1)
    tile_cls = jnp.minimum(jnp.sum((tile_ids[:, None] >= tile_end[None, :]).astype(jnp.int32), axis=1), N_CLASSES - 1)
    pair_lo = jnp.array([0, 0, 0, 1, 1, 2], jnp.int32)
    pair_hi = jnp.array([1, 2, 3, 2, 3, 3], jnp.int32)
    elo = (tile_cls // N_PAIRS) * EXPERTS_PER_GROUP + pair_lo[tile_cls % N_PAIRS]
    ehi = (tile_cls // N_PAIRS) * EXPERTS_PER_GROUP + pair_hi[tile_cls % N_PAIRS]
    p = n_t * tm
    src = jnp.full((p,), -1, jnp.int32).at[pos].set(jnp.arange(n, dtype=jnp.int32))
    nvalid = jnp.sum((src >= 0).reshape(n_t, tm), axis=1).astype(jnp.int32)
    src = jnp.maximum(src, 0)
    wts = jnp.zeros((p, 2), F32).at[pos].set(route[1:3].T)
    return elo, ehi, nvalid, src.reshape(n_t, 1, tm), wts


def kernel(x_prompt, x_sample, cache_sb_k, cache_sb_v, state_ret, meta, norm1_g, w_in, w_sb_o, w_ret_o,
           w_out, norm2_g, w_grp, b_grp, w_exp, b_exp, w_gate, w_up, w_down, normf_g):
    batch, seq, d = x_prompt.shape
    dec_b, dec_n, _ = x_sample.shape
    depth, _, past = cache_sb_k.shape[:3]
    assert depth == 1
    n_p, n_s = batch * seq, dec_b * dec_n
    sbw, qkw, vw = HEADS * SB_DH, HEADS * RET_DK, HEADS * RET_DV
    tm = min(ROW_TILE, n_s)

    xp = x_prompt.reshape(n_p, d)
    xs = x_sample.reshape(n_s, d)
    u = _norm_all(xp, xs, meta, norm1_g, tm)
    m_rows = u.shape[0]
    meta_row0 = n_p + n_s

    pos = jnp.concatenate([
        jnp.tile(jnp.arange(seq, dtype=F32), batch),
        jnp.tile(past + jnp.arange(dec_n, dtype=F32), dec_b),
        jnp.arange(m_rows - meta_row0, dtype=F32) - N_META,
    ])
    inv_freq = 1.0 / (ROPE_BASE ** (jnp.arange(0, RET_DK, 2, dtype=F32) / RET_DK))
    ang = pos[:, None] * inv_freq[None, :]
    cos_t = jnp.concatenate([jnp.cos(ang), jnp.cos(ang)], axis=1)
    sin_t = jnp.concatenate([-jnp.sin(ang), jnp.sin(ang)], axis=1)

    w_in16 = w_in[0].astype(BF16)
    proj = functools.partial(_project, u, w_in16, tm=tm, tn=COL_TILE)
    c = 0
    sb_q16 = proj(c, sbw, "bf16"); c += sbw
    sb_k32, sb_k16 = proj(c, sbw, "both"); c += sbw
    sb_v32, sb_v16 = proj(c, sbw, "both"); c += sbw
    r_q16 = proj(c, qkw, "rot", rot=(cos_t, sin_t), out_dtype=BF16); c += qkw
    r_k32 = proj(c, qkw, "rot", rot=(cos_t, sin_t), scale=RET_DK ** -0.5, out_dtype=F32); c += qkw
    r_v16 = proj(c, vw, "bf16"); c += vw
    gates = proj(c, vw + 2 * d, "f32")

    o_sb_p = _sb_prompt(sb_q16, sb_k16, sb_v16, batch, seq, meta_row0, min(SB_TILE, seq))
    o_sb_s = _sb_sample(sb_q16, sb_k16, sb_v16, cache_sb_k, cache_sb_v, dec_n, n_p, SB_TILE)

    log_g = jnp.log1p(-jnp.power(2.0, -5.0 - jnp.arange(HEADS, dtype=F32)))
    ret = functools.partial(_retention, log_g, r_q16, r_k32, r_v16, gates)
    zero_state = jnp.zeros((1, HEADS, RET_DK, RET_DV), F32)
    _, s_meta = ret(zero_state, 1, N_META, meta_row0, N_META, True)
    o_ret_p, s_p = ret(s_meta, batch, seq, 0, min(RET_CHUNK, seq), True)
    o_ret_s, s_s = ret(state_ret[0], dec_b, dec_n, n_p, dec_n, False)

    n_route = N_GROUPS + N_GROUPS * EXPERTS_PER_GROUP
    w_router = jnp.pad(jnp.concatenate([w_grp[0], w_exp[0]], axis=1), ((0, 0), (0, 128 - n_route)))
    b_router = jnp.pad(jnp.concatenate([b_grp[0], b_exp[0]]), (0, 128 - n_route)).reshape(1, 128)
    tmm = min(MERGE_TILE, n_s)
    merged = _gate_merge(o_sb_p, o_sb_s, o_ret_p, o_ret_s, gates,
                         w_sb_o[0].astype(BF16), w_ret_o[0].astype(BF16), tmm, COL_TILE)
    wr_hi = w_router.astype(BF16)
    wr_lo = (w_router - wr_hi.astype(F32)).astype(BF16)
    h, route = _out_proj(merged, xp, xs, w_out[0].astype(BF16), norm2_g, wr_hi, wr_lo, b_router, tmm)

    tmo = min(MOE_TILE, n_s)
    elo, ehi, nvalid, src, wts = _sort_by_class(route, tmo)
    y_p, y_s = _moe(h, elo, ehi, nvalid, src, wts, norm2_g, normf_g.reshape(1, d),
                    w_gate[0].astype(BF16), w_up[0].astype(BF16), w_down[0].astype(BF16), tmo, n_p)
    y_prompt = y_p.reshape(batch, seq, d)
    y_sample = y_s.reshape(dec_b, dec_n, d)

    def with_meta(a32):
        m = jnp.broadcast_to(a32[meta_row0:meta_row0 + N_META][None], (batch, N_META, sbw))
        full = jnp.concatenate([m, a32[:n_p].reshape(batch, seq, sbw)], axis=1)
        return full.reshape(1, batch, N_META + seq, HEADS, SB_DH)

    def sample_kv(a32):
        return a32[n_p:n_p + n_s].reshape(1, dec_b, dec_n, HEADS, SB_DH)

    return (y_prompt, y_sample, with_meta(sb_k32), with_meta(sb_v32), s_p[None],
            sample_kv(sb_k32), sample_kv(sb_v32), s_s[None])
```

```python
import functools

import jax
import jax.numpy as jnp
from jax import lax
from jax.experimental import pallas as pl
from jax.experimental.pallas import tpu as pltpu

F32 = jnp.float32
BF16 = jnp.bfloat16

N_META = 16
HEADS = 8
SB_DH = 128
RET_DK = 128
RET_DV = 256
N_GROUPS = 4
EXPERTS_PER_GROUP = 4
N_PAIRS = 6
N_CLASSES = N_GROUPS * N_PAIRS
ROPE_BASE = 10000.0
EPS = 1e-6
SB_LOG_CUTOFF = -88.0

ROW_TILE = 512
COL_TILE = 1024
MERGE_TILE = 512
MOE_TILE = 256
SB_TILE = 128
RET_CHUNK = 256
VMEM_LIMIT = 56 * 1024 * 1024


def _cparams(*sem):
    return pltpu.CompilerParams(dimension_semantics=sem, vmem_limit_bytes=VMEM_LIMIT)


def _rms(x, g):
    return (x * lax.rsqrt(jnp.mean(x * x, axis=-1, keepdims=True) + EPS)) * g


def _dot(a, b):
    return jnp.dot(a, b, preferred_element_type=F32)


def _dot_nt(a, b):
    return lax.dot_general(a, b, (((1,), (1,)), ((), ())), preferred_element_type=F32)


def _dot_tn(a, b):
    return lax.dot_general(a, b, (((0,), (0,)), ((), ())), preferred_element_type=F32)


def _norm_kernel(xp_ref, xs_ref, meta_ref, g_ref, u_ref, *, n_p, n_s):
    i = pl.program_id(0)

    @pl.when(i < n_p)
    def _():
        u_ref[...] = _rms(xp_ref[...], g_ref[...]).astype(BF16)

    @pl.when((i >= n_p) & (i < n_p + n_s))
    def _():
        u_ref[...] = _rms(xs_ref[...], g_ref[...]).astype(BF16)

    @pl.when(i >= n_p + n_s)
    def _():
        u_ref[...] = jnp.zeros_like(u_ref)
        u_ref[0:N_META, :] = _rms(meta_ref[...], g_ref[...]).astype(BF16)


def _norm_all(xp, xs, meta, g, tm):
    d = xp.shape[1]
    n_p, n_s = xp.shape[0] // tm, xs.shape[0] // tm
    return pl.pallas_call(
        functools.partial(_norm_kernel, n_p=n_p, n_s=n_s),
        grid=(n_p + n_s + 1,),
        in_specs=[
            pl.BlockSpec((tm, d), lambda i: (jnp.minimum(i, n_p - 1), 0)),
            pl.BlockSpec((tm, d), lambda i: (jnp.clip(i - n_p, 0, n_s - 1), 0)),
            pl.BlockSpec((N_META, d), lambda i: (0, 0)),
            pl.BlockSpec((1, d), lambda i: (0, 0)),
        ],
        out_specs=pl.BlockSpec((tm, d), lambda i: (i, 0)),
        out_shape=jax.ShapeDtypeStruct(((n_p + n_s + 1) * tm, d), BF16),
        compiler_params=_cparams("arbitrary"),
        name="norm1",
    )(xp, xs, meta, g)


def _proj_kernel(u_ref, w_ref, *refs, mode, scale):
    w16 = refs[-1]

    @pl.when(pl.program_id(1) == 0)
    def _():
        w16[...] = w_ref[...].astype(BF16)

    acc = _dot(u_ref[...], w16[...])
    if mode == "f32":
        refs[0][...] = acc
    elif mode == "bf16":
        refs[0][...] = acc.astype(BF16)
    else:
        cos_ref, sin_ref, o_ref = refs[:3]
        cos, sin = cos_ref[...], sin_ref[...]
        for hh in range(acc.shape[1] // RET_DK):
            x = acc[:, hh * RET_DK:(hh + 1) * RET_DK]
            r = x * cos + pltpu.roll(x, RET_DK // 2, 1) * sin
            if scale is not None:
                r = r * scale
            o_ref[:, hh * RET_DK:(hh + 1) * RET_DK] = r.astype(o_ref.dtype)


def _project(u, w, col0, width, mode, tm, tn, rot=None, scale=None, out_dtype=None):
    m, d = u.shape
    tn = min(tn, width)
    n_j, n_i = width // tn, m // tm
    j0 = col0 // tn
    in_specs = [
        pl.BlockSpec((tm, d), lambda j, i: (i, 0)),
        pl.BlockSpec((d, tn), lambda j, i: (0, j0 + j)),
    ]
    args = [u, w]
    out_specs = pl.BlockSpec((tm, tn), lambda j, i: (i, j))
    if mode == "rot":
        in_specs += [pl.BlockSpec((tm, RET_DK), lambda j, i: (i, 0))] * 2
        args += list(rot)
        out_shape = jax.ShapeDtypeStruct((m, width), out_dtype)
    else:
        out_shape = jax.ShapeDtypeStruct((m, width), F32 if mode == "f32" else BF16)
    return pl.pallas_call(
        functools.partial(_proj_kernel, mode=mode, scale=scale),
        grid=(n_j, n_i),
        in_specs=in_specs,
        out_specs=out_specs,
        out_shape=out_shape,
        scratch_shapes=[pltpu.VMEM((d, tn), BF16)],
        compiler_params=_cparams("arbitrary", "arbitrary"),
        name="proj_" + mode,
    )(*args)


def _proj_kv_kernel(u_ref, w_ref, o16_ref, outp_hbm, outs_hbm, w16, stage, sem,
                    *, tm, n_pt, n_st, tiles_per_batch, batch, rows_per_batch):
    i = pl.program_id(0)

    @pl.when(i == 0)
    def _():
        w16[...] = w_ref[...].astype(BF16)

    acc = _dot(u_ref[...], w16[...])
    o16_ref[...] = acc.astype(BF16)

    def tile_copy(dst):
        return pltpu.make_async_copy(stage, dst, sem)

    @pl.when(i > 0)
    def _():
        tile_copy(outp_hbm.at[pl.ds(0, tm)]).wait()

    for hd in range(HEADS):
        stage[:, hd, :] = acc[:, hd * SB_DH:(hd + 1) * SB_DH]

    @pl.when(i < n_pt)
    def _():
        b = i // tiles_per_batch
        r0 = (i % tiles_per_batch) * tm
        tile_copy(outp_hbm.at[pl.ds(b * rows_per_batch + N_META + r0, tm)]).start()

    @pl.when((i >= n_pt) & (i < n_pt + n_st))
    def _():
        tile_copy(outs_hbm.at[pl.ds((i - n_pt) * tm, tm)]).start()

    @pl.when(i == n_pt + n_st)
    def _():
        copies = [pltpu.make_async_copy(stage.at[pl.ds(0, N_META)],
                                        outp_hbm.at[pl.ds(b * rows_per_batch, N_META)], sem)
                  for b in range(batch)]
        for c in copies:
            c.start()
        for c in copies:
            c.wait()


def _project_kv(u, w, col0, tm, batch, seq, n_s):
    m, d = u.shape
    width = HEADS * SB_DH
    n_pt, n_st = batch * seq // tm, n_s // tm
    assert seq % tm == 0 and m // tm == n_pt + n_st + 1
    j0 = col0 // width
    rows_per_batch = N_META + seq
    return pl.pallas_call(
        functools.partial(_proj_kv_kernel, tm=tm, n_pt=n_pt, n_st=n_st, tiles_per_batch=seq // tm,
                          batch=batch, rows_per_batch=rows_per_batch),
        grid=(m // tm,),
        in_specs=[
            pl.BlockSpec((tm, d), lambda i: (i, 0)),
            pl.BlockSpec((d, width), lambda i: (0, j0), pipeline_mode=pl.Buffered(1)),
        ],
        out_specs=[
            pl.BlockSpec((tm, width), lambda i: (i, 0)),
            pl.BlockSpec(memory_space=pl.ANY),
            pl.BlockSpec(memory_space=pl.ANY),
        ],
        out_shape=[
            jax.ShapeDtypeStruct((m, width), BF16),
            jax.ShapeDtypeStruct((batch * rows_per_batch, HEADS, SB_DH), F32),
            jax.ShapeDtypeStruct((n_s, HEADS, SB_DH), F32),
        ],
        scratch_shapes=[
            pltpu.VMEM((d, width), BF16),
            pltpu.VMEM((tm, HEADS, SB_DH), F32),
            pltpu.SemaphoreType.DMA(()),
        ],
        compiler_params=_cparams("arbitrary"),
        name="proj_kv",
    )(u, w)


def _upper_ones(n):
    r = lax.broadcasted_iota(jnp.int32, (n, n), 0)
    c = lax.broadcasted_iota(jnp.int32, (n, n), 1)
    return (r > c).astype(BF16)


def _sb_tile(q, k, v, carry, acc, tri, valid=None):
    z = _dot_nt(q, k) * (SB_DH ** -0.5)
    sp = jnp.maximum(z, 0.0) + jnp.log(1.0 + jnp.exp(-jnp.abs(z)))
    lf = -sp if valid is None else jnp.where(valid, -sp, 0.0)
    hi = lf.astype(BF16)
    lo = (lf - hi.astype(F32)).astype(BF16)
    later = _dot(hi, tri) + _dot(lo, tri)
    a = jnp.exp((z - sp) + (later + carry))
    if valid is not None:
        a = jnp.where(valid, a, 0.0)
    acc = acc + _dot(a.astype(BF16), v)
    carry = carry + jnp.sum(lf, axis=-1, keepdims=True)
    return carry, acc


def _sb_prompt_kernel(q_ref, k_ref, v_ref, mk_ref, mv_ref, o_ref, *, t, n_q, unroll):
    tri = _upper_ones(t)
    tri2 = _upper_ones(2 * t)
    tri_m = _upper_ones(N_META)
    col_minus_row = (lax.broadcasted_iota(jnp.int32, (t, 2 * t), 1)
                     - lax.broadcasted_iota(jnp.int32, (t, 2 * t), 0))

    def window(i):
        r0 = pl.multiple_of(i * t, t)
        w0 = pl.multiple_of(jnp.maximum(i - 1, 0) * t, t)
        q = q_ref[pl.ds(r0, t), :]
        valid = col_minus_row < (r0 - w0)
        carry, acc = _sb_tile(q, k_ref[pl.ds(w0, 2 * t), :], v_ref[pl.ds(w0, 2 * t), :],
                              jnp.zeros((t, 1), F32), jnp.zeros((t, SB_DH), F32), tri2, valid)
        return q, carry, acc

    def older_keys(i, q, carry, acc):
        def cond(s):
            return (s[0] >= 0) & (jnp.max(s[1]) > SB_LOG_CUTOFF)

        def body(s):
            j, carry, acc = s
            c0 = pl.multiple_of(j * t, t)
            carry, acc = _sb_tile(q, k_ref[pl.ds(c0, t), :], v_ref[pl.ds(c0, t), :], carry, acc, tri)
            return j - 1, carry, acc

        _, carry, acc = lax.while_loop(cond, body, (i - 2, carry, acc))
        return lax.cond(
            jnp.max(carry) > SB_LOG_CUTOFF,
            lambda: _sb_tile(q, mk_ref[...], mv_ref[...], carry, acc, tri_m)[1],
            lambda: acc,
        )

    def group(p, _):
        tiles = [window(p * unroll + u) for u in range(unroll)]
        for u, (q, carry, acc) in enumerate(tiles):
            i = p * unroll + u
            acc = lax.cond(
                jnp.max(carry) > SB_LOG_CUTOFF,
                functools.partial(older_keys, i, q, carry, acc),
                lambda acc=acc: acc,
            )
            o_ref[pl.ds(pl.multiple_of(i * t, t), t), :] = acc.astype(o_ref.dtype)
        return 0

    lax.fori_loop(0, n_q // unroll, group, 0)


def _sb_prompt(q16, k16, v16, batch, seq, meta_row0, t):
    n_q = seq // t
    assert n_q >= 2
    unroll = 2 if n_q % 2 == 0 else 1
    mb = meta_row0 // N_META
    blk = lambda b, h: (b, h)
    mblk = lambda b, h: (mb, h)
    return pl.pallas_call(
        functools.partial(_sb_prompt_kernel, t=t, n_q=n_q, unroll=unroll),
        grid=(batch, HEADS),
        in_specs=[
            pl.BlockSpec((seq, SB_DH), blk),
            pl.BlockSpec((seq, SB_DH), blk),
            pl.BlockSpec((seq, SB_DH), blk),
            pl.BlockSpec((N_META, SB_DH), mblk),
            pl.BlockSpec((N_META, SB_DH), mblk),
        ],
        out_specs=pl.BlockSpec((seq, SB_DH), blk),
        out_shape=jax.ShapeDtypeStruct((batch * seq, HEADS * SB_DH), BF16),
        compiler_params=_cparams("arbitrary", "arbitrary"),
        name="sb_prompt",
    )(q16, k16, v16, k16, v16)


def _sb_sample_kernel(q_ref, k_ref, v_ref, ckl_ref, cvl_ref, ck_hbm, cv_hbm, o_ref, kbuf, vbuf, sem,
                      *, n, t, n_past):
    b = pl.program_id(0)
    tri_n = _upper_ones(n)
    tri = _upper_ones(t)
    causal = lax.broadcasted_iota(jnp.int32, (n, n), 1) < lax.broadcasted_iota(jnp.int32, (n, n), 0)

    def older_keys(hd, q, carry, acc):
        def cond(s):
            return (s[0] >= 0) & (jnp.max(s[1]) > SB_LOG_CUTOFF)

        def body(s):
            j, carry, acc = s
            rows = pl.ds(pl.multiple_of(j * t, t), t)
            ck = pltpu.make_async_copy(ck_hbm.at[0, b, rows], kbuf, sem.at[0])
            cv = pltpu.make_async_copy(cv_hbm.at[0, b, rows], vbuf, sem.at[1])
            ck.start()
            cv.start()
            ck.wait()
            cv.wait()
            carry, acc = _sb_tile(q, kbuf[:, hd, :].astype(BF16), vbuf[:, hd, :].astype(BF16), carry, acc, tri)
            return j - 1, carry, acc

        return lax.while_loop(cond, body, (n_past - 2, carry, acc))[2]

    heads = []
    for hd in range(HEADS):
        cols = slice(hd * SB_DH, (hd + 1) * SB_DH)
        q = q_ref[:, cols]
        carry, acc = _sb_tile(q, k_ref[:, cols], v_ref[:, cols], jnp.zeros((n, 1), F32),
                              jnp.zeros((n, SB_DH), F32), tri_n, causal)
        carry, acc = _sb_tile(q, ckl_ref[0, 0, :, hd, :].astype(BF16), cvl_ref[0, 0, :, hd, :].astype(BF16),
                              carry, acc, tri)
        heads.append((q, carry, acc))
    for hd, (q, carry, acc) in enumerate(heads):
        acc = lax.cond(
            jnp.max(carry) > SB_LOG_CUTOFF,
            functools.partial(older_keys, hd, q, carry, acc),
            lambda acc=acc: acc,
        )
        o_ref[:, hd * SB_DH:(hd + 1) * SB_DH] = acc.astype(o_ref.dtype)


def _sb_sample(q16, k16, v16, cache_k, cache_v, n, row0, t):
    _, dec_b, past = cache_k.shape[:3]
    t = min(t, past)
    n_past = past // t
    rb = row0 // n
    w = HEADS * SB_DH
    blk = lambda b: (rb + b, 0)
    last = lambda b: (0, b, n_past - 1, 0, 0)
    return pl.pallas_call(
        functools.partial(_sb_sample_kernel, n=n, t=t, n_past=n_past),
        grid=(dec_b,),
        in_specs=[
            pl.BlockSpec((n, w), blk),
            pl.BlockSpec((n, w), blk),
            pl.BlockSpec((n, w), blk),
            pl.BlockSpec((1, 1, t, HEADS, SB_DH), last),
            pl.BlockSpec((1, 1, t, HEADS, SB_DH), last),
            pl.BlockSpec(memory_space=pl.ANY),
            pl.BlockSpec(memory_space=pl.ANY),
        ],
        out_specs=pl.BlockSpec((n, w), lambda b: (b, 0)),
        out_shape=jax.ShapeDtypeStruct((dec_b * n, w), BF16),
        scratch_shapes=[
            pltpu.VMEM((t, HEADS, SB_DH), F32),
            pltpu.VMEM((t, HEADS, SB_DH), F32),
            pltpu.SemaphoreType.DMA((2,)),
        ],
        compiler_params=_cparams("arbitrary"),
        name="sb_sample",
    )(q16, k16, v16, cache_k, cache_v, cache_k, cache_v)


def _ret_kernel(lg_ref, q_ref, k_ref, v_ref, g_ref, s0_ref, o_ref, s_out_ref, s_scr, d_scr, *, c, n_c):
    ci = pl.program_id(1)

    @pl.when((pl.program_id(0) == 0) & (ci == 0))
    def _():
        rel = (lax.broadcasted_iota(jnp.int32, (c, c), 0) - lax.broadcasted_iota(jnp.int32, (c, c), 1)).astype(F32)
        for hd in range(HEADS):
            d_scr[hd] = jnp.where(rel >= 0, jnp.exp(jnp.maximum(rel, 0.0) * lg_ref[hd]), 0.0)

    @pl.when(ci == 0)
    def _():
        s_scr[...] = s0_ref[0]

    tcol = lax.broadcasted_iota(jnp.int32, (c, 1), 0).astype(F32)
    for hd in range(HEADS):
        lg = lg_ref[hd]
        qk = slice(hd * RET_DK, (hd + 1) * RET_DK)
        vv = slice(hd * RET_DV, (hd + 1) * RET_DV)
        q = q_ref[:, qk]
        kf = k_ref[:, qk]
        v = v_ref[:, vv]
        s = s_scr[hd]
        scores = _dot_nt(q, kf.astype(BF16)) * d_scr[hd]
        o = _dot(scores.astype(BF16), v) + _dot(q, s.astype(BF16)) * jnp.exp((tcol + 1.0) * lg)
        k_dec = (kf * jnp.exp((c - 1.0 - tcol) * lg)).astype(BF16)
        s_scr[hd] = jnp.exp(c * lg) * s + _dot_tn(k_dec, v)
        o = o * lax.rsqrt(jnp.mean(o * o, axis=-1, keepdims=True) + EPS)
        g = g_ref[:, vv]
        o_ref[:, vv] = (o * (g * jax.nn.sigmoid(g))).astype(o_ref.dtype)

    @pl.when(ci == n_c - 1)
    def _():
        s_out_ref[0] = s_scr[...]


def _retention(log_g, rq16, rk32, rv16, gates, s0, batch, rows_per_batch, row0, c, shared_s0):
    n_c = rows_per_batch // c
    rb = row0 // c
    qkw, vw = HEADS * RET_DK, HEADS * RET_DV
    row = lambda b, ci: (rb + b * n_c + ci, 0)
    sidx = (lambda b, ci: (0, 0, 0, 0)) if shared_s0 else (lambda b, ci: (b, 0, 0, 0))
    return pl.pallas_call(
        functools.partial(_ret_kernel, c=c, n_c=n_c),
        grid=(batch, n_c),
        in_specs=[
            pl.BlockSpec(memory_space=pltpu.SMEM),
            pl.BlockSpec((c, qkw), row),
            pl.BlockSpec((c, qkw), row),
            pl.BlockSpec((c, vw), row),
            pl.BlockSpec((c, vw), row),
            pl.BlockSpec((1, HEADS, RET_DK, RET_DV), sidx),
        ],
        out_specs=[
            pl.BlockSpec((c, vw), lambda b, ci: (b * n_c + ci, 0)),
            pl.BlockSpec((1, HEADS, RET_DK, RET_DV), lambda b, ci: (b, 0, 0, 0)),
        ],
        out_shape=[
            jax.ShapeDtypeStruct((batch * rows_per_batch, vw), BF16),
            jax.ShapeDtypeStruct((batch, HEADS, RET_DK, RET_DV), F32),
        ],
        scratch_shapes=[pltpu.VMEM((HEADS, RET_DK, RET_DV), F32), pltpu.VMEM((HEADS, c, c), F32)],
        compiler_params=_cparams("arbitrary", "arbitrary"),
        name="retention",
    )(log_g, rq16, rk32, rv16, gates, s0)


def _route(logits):
    gl = [logits[r:r + 1, :] for r in range(N_GROUPS)]
    gmax = functools.reduce(jnp.maximum, gl)
    g_idx = jnp.where(gl[0] == gmax, 0, jnp.where(gl[1] == gmax, 1, jnp.where(gl[2] == gmax, 2, 3)))
    g_top = 1.0 / functools.reduce(lambda a, b: a + b, [jnp.exp(x - gmax) for x in gl])
    e = []
    for r in range(EXPERTS_PER_GROUP):
        rows = [logits[N_GROUPS + g * EXPERTS_PER_GROUP + r:N_GROUPS + g * EXPERTS_PER_GROUP + r + 1, :]
                for g in range(N_GROUPS)]
        e.append(jnp.where(g_idx == 0, rows[0], jnp.where(g_idx == 1, rows[1], jnp.where(g_idx == 2, rows[2], rows[3]))))
    m1 = functools.reduce(jnp.maximum, e)
    i1 = jnp.where(e[0] == m1, 0, jnp.where(e[1] == m1, 1, jnp.where(e[2] == m1, 2, 3)))
    e2 = [jnp.where(i1 == r, -jnp.inf, e[r]) for r in range(EXPERTS_PER_GROUP)]
    m2 = functools.reduce(jnp.maximum, e2)
    i2 = jnp.where(e2[0] == m2, 0, jnp.where(e2[1] == m2, 1, jnp.where(e2[2] == m2, 2, 3)))
    p2 = jnp.exp(m2 - m1)
    w1 = g_top / (1.0 + p2)
    w2 = g_top * p2 / (1.0 + p2)
    lo = jnp.minimum(i1, i2)
    hi = jnp.maximum(i1, i2)
    w_lo = jnp.where(i1 < i2, w1, w2)
    w_hi = jnp.where(i1 < i2, w2, w1)
    pair = jnp.where(lo == 0, hi - 1, jnp.where(lo == 1, hi + 1, 5))
    return g_idx * N_PAIRS + pair, w_lo, w_hi


def _gate_merge_kernel(osp_ref, oss_ref, orp_ref, ors_ref, gsb_ref, gret_ref, wsb_ref, wret_ref, m_ref, *, n_p):
    is_p = pl.program_id(1) < n_p
    o_sb = jnp.where(is_p, osp_ref[...], oss_ref[...])
    o_ret = jnp.where(is_p, orp_ref[...], ors_ref[...])
    merged = (jax.nn.sigmoid(gsb_ref[...]) * _dot(o_sb, wsb_ref[...])
              + jax.nn.sigmoid(gret_ref[...]) * _dot(o_ret, wret_ref[...]))
    m_ref[...] = merged.astype(m_ref.dtype)


def _gate_merge(osp, oss, orp, ors, gates, wsb, wret, tm, tn):
    d = wsb.shape[1]
    tn = min(tn, d)
    n_p, n_s = osp.shape[0] // tm, oss.shape[0] // tm
    n_j = d // tn
    vw = HEADS * RET_DV
    pidx = lambda j, i: (jnp.minimum(i, n_p - 1), 0)
    sidx = lambda j, i: (jnp.clip(i - n_p, 0, n_s - 1), 0)
    return pl.pallas_call(
        functools.partial(_gate_merge_kernel, n_p=n_p),
        grid=(n_j, n_p + n_s),
        in_specs=[
            pl.BlockSpec((tm, HEADS * SB_DH), pidx),
            pl.BlockSpec((tm, HEADS * SB_DH), sidx),
            pl.BlockSpec((tm, vw), pidx),
            pl.BlockSpec((tm, vw), sidx),
            pl.BlockSpec((tm, tn), lambda j, i: (i, vw // tn + j)),
            pl.BlockSpec((tm, tn), lambda j, i: (i, (vw + d) // tn + j)),
            pl.BlockSpec((wsb.shape[0], tn), lambda j, i: (0, j)),
            pl.BlockSpec((wret.shape[0], tn), lambda j, i: (0, j)),
        ],
        out_specs=pl.BlockSpec((tm, tn), lambda j, i: (i, j)),
        out_shape=jax.ShapeDtypeStruct(((n_p + n_s) * tm, d), BF16),
        compiler_params=_cparams("arbitrary", "arbitrary"),
        name="gate_merge",
    )(osp, oss, orp, ors, gates, gates, wsb, wret)


def _out_kernel(m_ref, xp_ref, xs_ref, wout_ref, n2g_ref, wrh_ref, wrl_ref, br_ref, h_ref, route_ref, *, n_p):
    x = jnp.where(pl.program_id(0) < n_p, xp_ref[...], xs_ref[...])
    h = x + _dot(m_ref[...], wout_ref[...])
    h_ref[...] = h
    u2 = _rms(h, n2g_ref[...])
    u_hi = u2.astype(BF16)
    u_lo = (u2 - u_hi.astype(F32)).astype(BF16)
    logits = (_dot(u_hi, wrh_ref[...]) + (_dot(u_hi, wrl_ref[...]) + _dot(u_lo, wrh_ref[...]))) + br_ref[...]
    cls, w_lo, w_hi = _route(logits.T)
    route_ref[...] = jnp.zeros_like(route_ref)
    route_ref[0:1, :] = cls.astype(F32)
    route_ref[1:2, :] = w_lo
    route_ref[2:3, :] = w_hi


def _out_proj(m, xp, xs, wout, n2g, wr_hi, wr_lo, br, tm):
    d = xp.shape[1]
    n_p, n_s = xp.shape[0] // tm, xs.shape[0] // tm
    n = n_p + n_s
    const = lambda i: (0, 0)
    return pl.pallas_call(
        functools.partial(_out_kernel, n_p=n_p),
        grid=(n,),
        in_specs=[
            pl.BlockSpec((tm, d), lambda i: (i, 0)),
            pl.BlockSpec((tm, d), lambda i: (jnp.minimum(i, n_p - 1), 0)),
            pl.BlockSpec((tm, d), lambda i: (jnp.clip(i - n_p, 0, n_s - 1), 0)),
            pl.BlockSpec(wout.shape, const, pipeline_mode=pl.Buffered(1)),
            pl.BlockSpec((1, d), const),
            pl.BlockSpec(wr_hi.shape, const),
            pl.BlockSpec(wr_lo.shape, const),
            pl.BlockSpec(br.shape, const),
        ],
        out_specs=[
            pl.BlockSpec((tm, d), lambda i: (i, 0)),
            pl.BlockSpec((8, tm), lambda i: (0, i)),
        ],
        out_shape=[
            jax.ShapeDtypeStruct((n * tm, d), F32),
            jax.ShapeDtypeStruct((8, n * tm), F32),
        ],
        compiler_params=_cparams("arbitrary"),
        name="out_proj",
    )(m, xp, xs, wout, n2g, wr_hi, wr_lo, br)


def _moe_kernel(elo_ref, ehi_ref, nvalid_ref, src_ref, nxt_ref, wts_ref, h_hbm, n2g_ref, nfg_ref,
                wg_lo, wu_lo, wd_lo, wg_hi, wu_hi, wd_hi, yp_hbm, ys_hbm, hbuf, ybuf, trash, sem_in, sem_out,
                *, tm, n_p):
    t = pl.program_id(0)
    n_t = pl.num_programs(0)
    slot = t % 2
    n_valid = nvalid_ref[t]
    next_live = (t + 1 < n_t) & (nvalid_ref[jnp.minimum(t + 1, n_t - 1)] > 0)
    prev_live = (t > 0) & (nvalid_ref[jnp.maximum(t - 1, 0)] > 0)

    def start_gather(idx_ref, s):
        def row(r, c):
            pltpu.make_async_copy(h_hbm.at[pl.ds(idx_ref[0, 0, r], 1), :], hbuf.at[s, pl.ds(r, 1), :],
                                  sem_in.at[s]).start()
            return c

        lax.fori_loop(0, tm, row, 0, unroll=8)

    def wait_gather(s):
        pltpu.make_async_copy(h_hbm.at[pl.ds(0, tm), :], hbuf.at[s], sem_in.at[s]).wait()

    def wait_scatter(s):
        pltpu.make_async_copy(ybuf.at[s], yp_hbm.at[pl.ds(0, tm), :], sem_out.at[s]).wait()

    @pl.when(n_valid > 0)
    def _():
        @pl.when(t == 0)
        def _():
            start_gather(src_ref, 0)

        @pl.when(next_live)
        def _():
            start_gather(nxt_ref, 1 - slot)

        wait_gather(slot)
        hrows = hbuf[slot]
        u = _rms(hrows, n2g_ref[...]).astype(BF16)
        y = jnp.zeros_like(hrows)
        for col, (wg, wu, wd) in enumerate(((wg_lo, wu_lo, wd_lo), (wg_hi, wu_hi, wd_hi))):
            gate = _dot(u, wg[0])
            hid = (gate * jax.nn.sigmoid(gate)) * _dot(u, wu[0])
            y = y + wts_ref[:, col:col + 1] * _dot(hid.astype(BF16), wd[0])
        ybuf[slot] = _rms(hrows + y, nfg_ref[...])

        @pl.when(prev_live)
        def _():
            wait_scatter(1 - slot)

        def scatter(r, c):
            row = src_ref[0, 0, r]
            src = ybuf.at[slot, pl.ds(r, 1), :]
            real = r < n_valid

            @pl.when(real & (row < n_p))
            def _():
                pltpu.make_async_copy(src, yp_hbm.at[pl.ds(row, 1), :], sem_out.at[slot]).start()

            @pl.when(real & (row >= n_p))
            def _():
                pltpu.make_async_copy(src, ys_hbm.at[pl.ds(row - n_p, 1), :], sem_out.at[slot]).start()

            @pl.when(jnp.logical_not(real))
            def _():
                pltpu.make_async_copy(src, trash.at[pl.ds(r, 1), :], sem_out.at[slot]).start()

            return c

        lax.fori_loop(0, tm, scatter, 0, unroll=8)

        @pl.when(jnp.logical_not(next_live))
        def _():
            wait_scatter(slot)


def _moe(h, elo, ehi, nvalid, src, wts, n2g, nfg, wg, wu, wd, tm, n_p):
    n, d = h.shape
    f = wg.shape[-1]
    n_t = src.shape[0]
    lo = lambda t, elo, ehi, nu: (elo[t], 0, 0)
    hi = lambda t, elo, ehi, nu: (ehi[t], 0, 0)
    const = lambda t, elo, ehi, nu: (0, 0)
    grid_spec = pltpu.PrefetchScalarGridSpec(
        num_scalar_prefetch=3,
        grid=(n_t,),
        in_specs=[
            pl.BlockSpec((1, 1, tm), lambda t, *_: (t, 0, 0), memory_space=pltpu.SMEM),
            pl.BlockSpec((1, 1, tm), lambda t, *_: (jnp.minimum(t + 1, n_t - 1), 0, 0), memory_space=pltpu.SMEM),
            pl.BlockSpec((tm, 2), lambda t, *_: (t, 0)),
            pl.BlockSpec(memory_space=pl.ANY),
            pl.BlockSpec((1, d), const),
            pl.BlockSpec((1, d), const),
            pl.BlockSpec((1, d, f), lo),
            pl.BlockSpec((1, d, f), lo),
            pl.BlockSpec((1, f, d), lo),
            pl.BlockSpec((1, d, f), hi),
            pl.BlockSpec((1, d, f), hi),
            pl.BlockSpec((1, f, d), hi),
        ],
        out_specs=[pl.BlockSpec(memory_space=pl.ANY), pl.BlockSpec(memory_space=pl.ANY)],
        scratch_shapes=[
            pltpu.VMEM((2, tm, d), F32),
            pltpu.VMEM((2, tm, d), F32),
            pltpu.VMEM((tm, d), F32),
            pltpu.SemaphoreType.DMA((2,)),
            pltpu.SemaphoreType.DMA((2,)),
        ],
    )
    return pl.pallas_call(
        functools.partial(_moe_kernel, tm=tm, n_p=n_p),
        grid_spec=grid_spec,
        out_shape=[jax.ShapeDtypeStruct((n_p, d), F32), jax.ShapeDtypeStruct((n - n_p, d), F32)],
        compiler_params=_cparams("arbitrary"),
        name="moe",
    )(elo, ehi, nvalid, src, src, wts, h, n2g, nfg, wg, wu, wd, wg, wu, wd)


def _sort_by_class(route, tm):
    n = route.shape[1]
    n_t = n // tm + N_CLASSES
    cls = route[0].astype(jnp.int32)
    onehot = (cls[:, None] == jnp.arange(N_CLASSES, dtype=jnp.int32)[None, :]).astype(jnp.int32)
    rank = jnp.sum((jnp.cumsum(onehot, axis=0) - onehot) * onehot, axis=1)
    tiles_per = (jnp.sum(onehot, axis=0) + tm - 1) // tm
    tile_end = jnp.cumsum(tiles_per)
    nused = tile_end[-1]
    pos = (tile_end - tiles_per)[cls] * tm + rank
    tile_ids = jnp.minimum(jnp.arange(n_t, dtype=jnp.int32), nused - 1)
    tile_cls = jnp.minimum(jnp.sum((tile_ids[:, None] >= tile_end[None, :]).astype(jnp.int32), axis=1), N_CLASSES - 1)
    pair_lo = jnp.array([0, 0, 0, 1, 1, 2], jnp.int32)
    pair_hi = jnp.array([1, 2, 3, 2, 3, 3], jnp.int32)
    elo = (tile_cls // N_PAIRS) * EXPERTS_PER_GROUP + pair_lo[tile_cls % N_PAIRS]
    ehi = (tile_cls // N_PAIRS) * EXPERTS_PER_GROUP + pair_hi[tile_cls % N_PAIRS]
    p = n_t * tm
    src = jnp.full((p,), -1, jnp.int32).at[pos].set(jnp.arange(n, dtype=jnp.int32))
    nvalid = jnp.sum((src >= 0).reshape(n_t, tm), axis=1).astype(jnp.int32)
    src = jnp.maximum(src, 0)
    wts = jnp.take(route[1:3].T, src, axis=0)
    return elo, ehi, nvalid, src.reshape(n_t, 1, tm), wts


def kernel(x_prompt, x_sample, cache_sb_k, cache_sb_v, state_ret, meta, norm1_g, w_in, w_sb_o, w_ret_o,
           w_out, norm2_g, w_grp, b_grp, w_exp, b_exp, w_gate, w_up, w_down, normf_g):
    batch, seq, d = x_prompt.shape
    dec_b, dec_n, _ = x_sample.shape
    depth, _, past = cache_sb_k.shape[:3]
    assert depth == 1
    n_p, n_s = batch * seq, dec_b * dec_n
    sbw, qkw, vw = HEADS * SB_DH, HEADS * RET_DK, HEADS * RET_DV
    tm = min(ROW_TILE, n_s)

    xp = x_prompt.reshape(n_p, d)
    xs = x_sample.reshape(n_s, d)
    u = _norm_all(xp, xs, meta, norm1_g, tm)
    m_rows = u.shape[0]
    meta_row0 = n_p + n_s

    pos = jnp.concatenate([
        jnp.tile(jnp.arange(seq, dtype=F32), batch),
        jnp.tile(past + jnp.arange(dec_n, dtype=F32), dec_b),
        jnp.arange(m_rows - meta_row0, dtype=F32) - N_META,
    ])
    inv_freq = 1.0 / (ROPE_BASE ** (jnp.arange(0, RET_DK, 2, dtype=F32) / RET_DK))
    ang = pos[:, None] * inv_freq[None, :]
    cos_t = jnp.concatenate([jnp.cos(ang), jnp.cos(ang)], axis=1)
    sin_t = jnp.concatenate([-jnp.sin(ang), jnp.sin(ang)], axis=1)

    proj = functools.partial(_project, u, w_in[0], tm=tm, tn=COL_TILE)
    c = 0
    sb_q16 = proj(c, sbw, "bf16"); c += sbw
    sb_k16, k_prompt, k_sample = _project_kv(u, w_in[0], c, tm, batch, seq, n_s); c += sbw
    sb_v16, v_prompt, v_sample = _project_kv(u, w_in[0], c, tm, batch, seq, n_s); c += sbw
    r_q16 = proj(c, qkw, "rot", rot=(cos_t, sin_t), out_dtype=BF16); c += qkw
    r_k32 = proj(c, qkw, "rot", rot=(cos_t, sin_t), scale=RET_DK ** -0.5, out_dtype=F32); c += qkw
    r_v16 = proj(c, vw, "bf16"); c += vw
    gates = proj(c, vw + 2 * d, "f32")

    o_sb_p = _sb_prompt(sb_q16, sb_k16, sb_v16, batch, seq, meta_row0, min(SB_TILE, seq))
    o_sb_s = _sb_sample(sb_q16, sb_k16, sb_v16, cache_sb_k, cache_sb_v, dec_n, n_p, SB_TILE)

    log_g = jnp.log1p(-jnp.power(2.0, -5.0 - jnp.arange(HEADS, dtype=F32)))
    ret = functools.partial(_retention, log_g, r_q16, r_k32, r_v16, gates)
    zero_state = jnp.zeros((1, HEADS, RET_DK, RET_DV), F32)
    _, s_meta = ret(zero_state, 1, N_META, meta_row0, N_META, True)
    o_ret_p, s_p = ret(s_meta, batch, seq, 0, min(RET_CHUNK, seq), True)
    o_ret_s, s_s = ret(state_ret[0], dec_b, dec_n, n_p, dec_n, False)

    n_route = N_GROUPS + N_GROUPS * EXPERTS_PER_GROUP
    w_router = jnp.pad(jnp.concatenate([w_grp[0], w_exp[0]], axis=1), ((0, 0), (0, 128 - n_route)))
    b_router = jnp.pad(jnp.concatenate([b_grp[0], b_exp[0]]), (0, 128 - n_route)).reshape(1, 128)
    tmm = min(MERGE_TILE, n_s)
    merged = _gate_merge(o_sb_p, o_sb_s, o_ret_p, o_ret_s, gates,
                         w_sb_o[0].astype(BF16), w_ret_o[0].astype(BF16), tmm, COL_TILE)
    wr_hi = w_router.astype(BF16)
    wr_lo = (w_router - wr_hi.astype(F32)).astype(BF16)
    h, route = _out_proj(merged, xp, xs, w_out[0].astype(BF16), norm2_g, wr_hi, wr_lo, b_router, tmm)

    tmo = min(MOE_TILE, n_s)
    elo, ehi, nvalid, src, wts = _sort_by_class(route, tmo)
    y_p, y_s = _moe(h, elo, ehi, nvalid, src, wts, norm2_g, normf_g.reshape(1, d),
                    w_gate[0].astype(BF16), w_up[0].astype(BF16), w_down[0].astype(BF16), tmo, n_p)
    y_prompt = y_p.reshape(batch, seq, d)
    y_sample = y_s.reshape(dec_b, dec_n, d)

    pshape = (1, batch, N_META + seq, HEADS, SB_DH)
    sshape = (1, dec_b, dec_n, HEADS, SB_DH)
    return (y_prompt, y_sample, k_prompt.reshape(pshape), v_prompt.reshape(pshape), s_p[None],
            k_sample.reshape(sshape), v_sample.reshape(sshape), s_s[None])
```

```python
import functools

import jax
import jax.numpy as jnp
from jax import lax
from jax.experimental import pallas as pl
from jax.experimental.pallas import tpu as pltpu

F32 = jnp.float32
BF16 = jnp.bfloat16

N_META = 16
HEADS = 8
SB_DH = 128
RET_DK = 128
RET_DV = 256
N_GROUPS = 4
EXPERTS_PER_GROUP = 4
N_PAIRS = 6
N_CLASSES = N_GROUPS * N_PAIRS
ROPE_BASE = 10000.0
EPS = 1e-6
SB_LOG_CUTOFF = -88.0

ROW_TILE = 512
COL_TILE = 1024
MERGE_TILE = 512
MOE_TILE = 256
SCATTER_CHUNK = 8
SB_TILE = 128
RET_CHUNK = 256
VMEM_LIMIT = 56 * 1024 * 1024


def _cparams(*sem):
    return pltpu.CompilerParams(dimension_semantics=sem, vmem_limit_bytes=VMEM_LIMIT)


def _rms(x, g):
    return (x * lax.rsqrt(jnp.mean(x * x, axis=-1, keepdims=True) + EPS)) * g


def _dot(a, b):
    return jnp.dot(a, b, preferred_element_type=F32)


def _dot_nt(a, b):
    return lax.dot_general(a, b, (((1,), (1,)), ((), ())), preferred_element_type=F32)


def _dot_tn(a, b):
    return lax.dot_general(a, b, (((0,), (0,)), ((), ())), preferred_element_type=F32)


def _norm_kernel(xp_ref, xs_ref, meta_ref, g_ref, u_ref, *, n_p, n_s):
    i = pl.program_id(0)

    @pl.when(i < n_p)
    def _():
        u_ref[...] = _rms(xp_ref[...], g_ref[...]).astype(BF16)

    @pl.when((i >= n_p) & (i < n_p + n_s))
    def _():
        u_ref[...] = _rms(xs_ref[...], g_ref[...]).astype(BF16)

    @pl.when(i >= n_p + n_s)
    def _():
        u_ref[...] = jnp.zeros_like(u_ref)
        u_ref[0:N_META, :] = _rms(meta_ref[...], g_ref[...]).astype(BF16)


def _norm_all(xp, xs, meta, g, tm):
    d = xp.shape[1]
    n_p, n_s = xp.shape[0] // tm, xs.shape[0] // tm
    return pl.pallas_call(
        functools.partial(_norm_kernel, n_p=n_p, n_s=n_s),
        grid=(n_p + n_s + 1,),
        in_specs=[
            pl.BlockSpec((tm, d), lambda i: (jnp.minimum(i, n_p - 1), 0)),
            pl.BlockSpec((tm, d), lambda i: (jnp.clip(i - n_p, 0, n_s - 1), 0)),
            pl.BlockSpec((N_META, d), lambda i: (0, 0)),
            pl.BlockSpec((1, d), lambda i: (0, 0)),
        ],
        out_specs=pl.BlockSpec((tm, d), lambda i: (i, 0)),
        out_shape=jax.ShapeDtypeStruct(((n_p + n_s + 1) * tm, d), BF16),
        compiler_params=_cparams("arbitrary"),
        name="norm1",
    )(xp, xs, meta, g)


def _proj_kernel(u_ref, w_ref, *refs, mode, scale):
    w16 = refs[-1]

    @pl.when(pl.program_id(1) == 0)
    def _():
        w16[...] = w_ref[...].astype(BF16)

    acc = _dot(u_ref[...], w16[...])
    if mode == "f32":
        refs[0][...] = acc
    elif mode == "bf16":
        refs[0][...] = acc.astype(BF16)
    else:
        cos_ref, sin_ref, o_ref = refs[:3]
        cos, sin = cos_ref[...], sin_ref[...]
        for hh in range(acc.shape[1] // RET_DK):
            x = acc[:, hh * RET_DK:(hh + 1) * RET_DK]
            r = x * cos + pltpu.roll(x, RET_DK // 2, 1) * sin
            if scale is not None:
                r = r * scale
            o_ref[:, hh * RET_DK:(hh + 1) * RET_DK] = r.astype(o_ref.dtype)


def _project(u, w, col0, width, mode, tm, tn, rot=None, scale=None, out_dtype=None):
    m, d = u.shape
    tn = min(tn, width)
    n_j, n_i = width // tn, m // tm
    j0 = col0 // tn
    in_specs = [
        pl.BlockSpec((tm, d), lambda j, i: (i, 0)),
        pl.BlockSpec((d, tn), lambda j, i: (0, j0 + j)),
    ]
    args = [u, w]
    out_specs = pl.BlockSpec((tm, tn), lambda j, i: (i, j))
    if mode == "rot":
        in_specs += [pl.BlockSpec((tm, RET_DK), lambda j, i: (i, 0))] * 2
        args += list(rot)
        out_shape = jax.ShapeDtypeStruct((m, width), out_dtype)
    else:
        out_shape = jax.ShapeDtypeStruct((m, width), F32 if mode == "f32" else BF16)
    return pl.pallas_call(
        functools.partial(_proj_kernel, mode=mode, scale=scale),
        grid=(n_j, n_i),
        in_specs=in_specs,
        out_specs=out_specs,
        out_shape=out_shape,
        scratch_shapes=[pltpu.VMEM((d, tn), BF16)],
        compiler_params=_cparams("arbitrary", "arbitrary"),
        name="proj_" + mode,
    )(*args)


def _proj_kv_kernel(u_ref, w_ref, o16_ref, outp_hbm, outs_hbm, w16, stage, sem,
                    *, tm, n_pt, n_st, tiles_per_batch, batch, rows_per_batch):
    i = pl.program_id(0)

    @pl.when(i == 0)
    def _():
        w16[...] = w_ref[...].astype(BF16)

    acc = _dot(u_ref[...], w16[...])
    o16_ref[...] = acc.astype(BF16)

    def tile_copy(dst):
        return pltpu.make_async_copy(stage, dst, sem)

    @pl.when(i > 0)
    def _():
        tile_copy(outp_hbm.at[pl.ds(0, tm)]).wait()

    for hd in range(HEADS):
        stage[:, hd, :] = acc[:, hd * SB_DH:(hd + 1) * SB_DH]

    @pl.when(i < n_pt)
    def _():
        b = i // tiles_per_batch
        r0 = (i % tiles_per_batch) * tm
        tile_copy(outp_hbm.at[pl.ds(b * rows_per_batch + N_META + r0, tm)]).start()

    @pl.when((i >= n_pt) & (i < n_pt + n_st))
    def _():
        tile_copy(outs_hbm.at[pl.ds((i - n_pt) * tm, tm)]).start()

    @pl.when(i == n_pt + n_st)
    def _():
        copies = [pltpu.make_async_copy(stage.at[pl.ds(0, N_META)],
                                        outp_hbm.at[pl.ds(b * rows_per_batch, N_META)], sem)
                  for b in range(batch)]
        for c in copies:
            c.start()
        for c in copies:
            c.wait()


def _project_kv(u, w, col0, tm, batch, seq, n_s):
    m, d = u.shape
    width = HEADS * SB_DH
    n_pt, n_st = batch * seq // tm, n_s // tm
    assert seq % tm == 0 and m // tm == n_pt + n_st + 1
    j0 = col0 // width
    rows_per_batch = N_META + seq
    return pl.pallas_call(
        functools.partial(_proj_kv_kernel, tm=tm, n_pt=n_pt, n_st=n_st, tiles_per_batch=seq // tm,
                          batch=batch, rows_per_batch=rows_per_batch),
        grid=(m // tm,),
        in_specs=[
            pl.BlockSpec((tm, d), lambda i: (i, 0)),
            pl.BlockSpec((d, width), lambda i: (0, j0), pipeline_mode=pl.Buffered(1)),
        ],
        out_specs=[
            pl.BlockSpec((tm, width), lambda i: (i, 0)),
            pl.BlockSpec(memory_space=pl.ANY),
            pl.BlockSpec(memory_space=pl.ANY),
        ],
        out_shape=[
            jax.ShapeDtypeStruct((m, width), BF16),
            jax.ShapeDtypeStruct((batch * rows_per_batch, HEADS, SB_DH), F32),
            jax.ShapeDtypeStruct((n_s, HEADS, SB_DH), F32),
        ],
        scratch_shapes=[
            pltpu.VMEM((d, width), BF16),
            pltpu.VMEM((tm, HEADS, SB_DH), F32),
            pltpu.SemaphoreType.DMA(()),
        ],
        compiler_params=_cparams("arbitrary"),
        name="proj_kv",
    )(u, w)


def _upper_ones(n):
    r = lax.broadcasted_iota(jnp.int32, (n, n), 0)
    c = lax.broadcasted_iota(jnp.int32, (n, n), 1)
    return (r > c).astype(BF16)


def _sb_tiles(tiles, tri):
    n = range(len(tiles))
    q, k, v, carry, acc, valid = zip(*tiles)
    z = [_dot_nt(q[i], k[i]) * (SB_DH ** -0.5) for i in n]
    e = [jnp.exp(-jnp.abs(z[i])) for i in n]
    sp = [jnp.maximum(z[i], 0.0) + jnp.log(1.0 + e[i]) for i in n]
    lf = [-sp[i] if valid[i] is None else jnp.where(valid[i], -sp[i], 0.0) for i in n]
    hi = [lf[i].astype(BF16) for i in n]
    lo = [(lf[i] - hi[i].astype(F32)).astype(BF16) for i in n]
    later = [_dot(hi[i], tri) + _dot(lo[i], tri) for i in n]
    a = [jnp.exp((z[i] - sp[i]) + (later[i] + carry[i])) for i in n]
    a = [a[i] if valid[i] is None else jnp.where(valid[i], a[i], 0.0) for i in n]
    acc = [acc[i] + _dot(a[i].astype(BF16), v[i]) for i in n]
    carry = [carry[i] + jnp.sum(lf[i], axis=-1, keepdims=True) for i in n]
    return list(zip(carry, acc))


def _sb_tile(q, k, v, carry, acc, tri, valid=None):
    return _sb_tiles([(q, k, v, carry, acc, valid)], tri)[0]


def _sb_prompt_kernel(q_ref, k_ref, v_ref, mk_ref, mv_ref, o_ref, *, t, n_q, unroll):
    tri = _upper_ones(t)
    tri2 = _upper_ones(2 * t)
    tri_m = _upper_ones(N_META)
    col_minus_row = (lax.broadcasted_iota(jnp.int32, (t, 2 * t), 1)
                     - lax.broadcasted_iota(jnp.int32, (t, 2 * t), 0))

    def window(i):
        r0 = pl.multiple_of(i * t, t)
        w0 = pl.multiple_of(jnp.maximum(i - 1, 0) * t, t)
        valid = col_minus_row < (r0 - w0)
        return (q_ref[pl.ds(r0, t), :], k_ref[pl.ds(w0, 2 * t), :], v_ref[pl.ds(w0, 2 * t), :],
                jnp.zeros((t, 1), F32), jnp.zeros((t, SB_DH), F32), valid)

    def older_keys(i, q, carry, acc):
        def cond(s):
            return (s[0] >= 0) & (jnp.max(s[1]) > SB_LOG_CUTOFF)

        def body(s):
            j, carry, acc = s
            c0 = pl.multiple_of(j * t, t)
            carry, acc = _sb_tile(q, k_ref[pl.ds(c0, t), :], v_ref[pl.ds(c0, t), :], carry, acc, tri)
            return j - 1, carry, acc

        _, carry, acc = lax.while_loop(cond, body, (i - 2, carry, acc))
        return lax.cond(
            jnp.max(carry) > SB_LOG_CUTOFF,
            lambda: _sb_tile(q, mk_ref[...], mv_ref[...], carry, acc, tri_m)[1],
            lambda: acc,
        )

    def group(p, _):
        tiles = [window(p * unroll + u) for u in range(unroll)]
        done = _sb_tiles(tiles, tri2)
        for u, (carry, acc) in enumerate(done):
            i = p * unroll + u
            q = tiles[u][0]
            acc = lax.cond(
                jnp.max(carry) > SB_LOG_CUTOFF,
                functools.partial(older_keys, i, q, carry, acc),
                lambda acc=acc: acc,
            )
            o_ref[pl.ds(pl.multiple_of(i * t, t), t), :] = acc.astype(o_ref.dtype)
        return 0

    lax.fori_loop(0, n_q // unroll, group, 0)


def _sb_prompt(q16, k16, v16, batch, seq, meta_row0, t):
    n_q = seq // t
    assert n_q >= 2
    unroll = max(u for u in (1, 2, 4) if n_q % u == 0)
    mb = meta_row0 // N_META
    blk = lambda b, h: (b, h)
    mblk = lambda b, h: (mb, h)
    return pl.pallas_call(
        functools.partial(_sb_prompt_kernel, t=t, n_q=n_q, unroll=unroll),
        grid=(batch, HEADS),
        in_specs=[
            pl.BlockSpec((seq, SB_DH), blk),
            pl.BlockSpec((seq, SB_DH), blk),
            pl.BlockSpec((seq, SB_DH), blk),
            pl.BlockSpec((N_META, SB_DH), mblk),
            pl.BlockSpec((N_META, SB_DH), mblk),
        ],
        out_specs=pl.BlockSpec((seq, SB_DH), blk),
        out_shape=jax.ShapeDtypeStruct((batch * seq, HEADS * SB_DH), BF16),
        compiler_params=_cparams("arbitrary", "arbitrary"),
        name="sb_prompt",
    )(q16, k16, v16, k16, v16)


def _sb_sample_kernel(q_ref, k_ref, v_ref, ckl_ref, cvl_ref, ck_hbm, cv_hbm, o_ref, kbuf, vbuf, sem,
                      *, n, t, n_past):
    b = pl.program_id(0)
    tri_n = _upper_ones(n)
    tri = _upper_ones(t)
    causal = lax.broadcasted_iota(jnp.int32, (n, n), 1) < lax.broadcasted_iota(jnp.int32, (n, n), 0)

    def older_keys(hd, q, carry, acc):
        def cond(s):
            return (s[0] >= 0) & (jnp.max(s[1]) > SB_LOG_CUTOFF)

        def body(s):
            j, carry, acc = s
            rows = pl.ds(pl.multiple_of(j * t, t), t)
            ck = pltpu.make_async_copy(ck_hbm.at[0, b, rows], kbuf, sem.at[0])
            cv = pltpu.make_async_copy(cv_hbm.at[0, b, rows], vbuf, sem.at[1])
            ck.start()
            cv.start()
            ck.wait()
            cv.wait()
            carry, acc = _sb_tile(q, kbuf[:, hd, :].astype(BF16), vbuf[:, hd, :].astype(BF16), carry, acc, tri)
            return j - 1, carry, acc

        return lax.while_loop(cond, body, (n_past - 2, carry, acc))[2]

    cols = [slice(hd * SB_DH, (hd + 1) * SB_DH) for hd in range(HEADS)]
    qs = [q_ref[:, c] for c in cols]
    new = _sb_tiles([(qs[hd], k_ref[:, cols[hd]], v_ref[:, cols[hd]], jnp.zeros((n, 1), F32),
                      jnp.zeros((n, SB_DH), F32), causal) for hd in range(HEADS)], tri_n)
    old = _sb_tiles([(qs[hd], ckl_ref[0, 0, :, hd, :].astype(BF16), cvl_ref[0, 0, :, hd, :].astype(BF16),
                      new[hd][0], new[hd][1], None) for hd in range(HEADS)], tri)
    for hd, (carry, acc) in enumerate(old):
        q = qs[hd]
        acc = lax.cond(
            jnp.max(carry) > SB_LOG_CUTOFF,
            functools.partial(older_keys, hd, q, carry, acc),
            lambda acc=acc: acc,
        )
        o_ref[:, hd * SB_DH:(hd + 1) * SB_DH] = acc.astype(o_ref.dtype)


def _sb_sample(q16, k16, v16, cache_k, cache_v, n, row0, t):
    _, dec_b, past = cache_k.shape[:3]
    t = min(t, past)
    n_past = past // t
    rb = row0 // n
    w = HEADS * SB_DH
    blk = lambda b: (rb + b, 0)
    last = lambda b: (0, b, n_past - 1, 0, 0)
    return pl.pallas_call(
        functools.partial(_sb_sample_kernel, n=n, t=t, n_past=n_past),
        grid=(dec_b,),
        in_specs=[
            pl.BlockSpec((n, w), blk),
            pl.BlockSpec((n, w), blk),
            pl.BlockSpec((n, w), blk),
            pl.BlockSpec((1, 1, t, HEADS, SB_DH), last),
            pl.BlockSpec((1, 1, t, HEADS, SB_DH), last),
            pl.BlockSpec(memory_space=pl.ANY),
            pl.BlockSpec(memory_space=pl.ANY),
        ],
        out_specs=pl.BlockSpec((n, w), lambda b: (b, 0)),
        out_shape=jax.ShapeDtypeStruct((dec_b * n, w), BF16),
        scratch_shapes=[
            pltpu.VMEM((t, HEADS, SB_DH), F32),
            pltpu.VMEM((t, HEADS, SB_DH), F32),
            pltpu.SemaphoreType.DMA((2,)),
        ],
        compiler_params=_cparams("arbitrary"),
        name="sb_sample",
    )(q16, k16, v16, cache_k, cache_v, cache_k, cache_v)


def _ret_kernel(lg_ref, q_ref, k_ref, v_ref, g_ref, s0_ref, o_ref, s_out_ref, s_scr, d_scr, *, c, n_c):
    ci = pl.program_id(1)

    @pl.when((pl.program_id(0) == 0) & (ci == 0))
    def _():
        rel = (lax.broadcasted_iota(jnp.int32, (c, c), 0) - lax.broadcasted_iota(jnp.int32, (c, c), 1)).astype(F32)
        for hd in range(HEADS):
            d_scr[hd] = jnp.where(rel >= 0, jnp.exp(jnp.maximum(rel, 0.0) * lg_ref[hd]), 0.0)

    @pl.when(ci == 0)
    def _():
        s_scr[...] = s0_ref[0]

    tcol = lax.broadcasted_iota(jnp.int32, (c, 1), 0).astype(F32)
    for hd in range(HEADS):
        lg = lg_ref[hd]
        qk = slice(hd * RET_DK, (hd + 1) * RET_DK)
        vv = slice(hd * RET_DV, (hd + 1) * RET_DV)
        q = q_ref[:, qk]
        kf = k_ref[:, qk]
        v = v_ref[:, vv]
        s = s_scr[hd]
        scores = _dot_nt(q, kf.astype(BF16)) * d_scr[hd]
        o = _dot(scores.astype(BF16), v) + _dot(q, s.astype(BF16)) * jnp.exp((tcol + 1.0) * lg)
        k_dec = (kf * jnp.exp((c - 1.0 - tcol) * lg)).astype(BF16)
        s_scr[hd] = jnp.exp(c * lg) * s + _dot_tn(k_dec, v)
        o = o * lax.rsqrt(jnp.mean(o * o, axis=-1, keepdims=True) + EPS)
        g = g_ref[:, vv]
        o_ref[:, vv] = (o * (g * jax.nn.sigmoid(g))).astype(o_ref.dtype)

    @pl.when(ci == n_c - 1)
    def _():
        s_out_ref[0] = s_scr[...]


def _retention(log_g, rq16, rk32, rv16, gates, s0, batch, rows_per_batch, row0, c, shared_s0):
    n_c = rows_per_batch // c
    rb = row0 // c
    qkw, vw = HEADS * RET_DK, HEADS * RET_DV
    row = lambda b, ci: (rb + b * n_c + ci, 0)
    sidx = (lambda b, ci: (0, 0, 0, 0)) if shared_s0 else (lambda b, ci: (b, 0, 0, 0))
    return pl.pallas_call(
        functools.partial(_ret_kernel, c=c, n_c=n_c),
        grid=(batch, n_c),
        in_specs=[
            pl.BlockSpec(memory_space=pltpu.SMEM),
            pl.BlockSpec((c, qkw), row),
            pl.BlockSpec((c, qkw), row),
            pl.BlockSpec((c, vw), row),
            pl.BlockSpec((c, vw), row),
            pl.BlockSpec((1, HEADS, RET_DK, RET_DV), sidx),
        ],
        out_specs=[
            pl.BlockSpec((c, vw), lambda b, ci: (b * n_c + ci, 0)),
            pl.BlockSpec((1, HEADS, RET_DK, RET_DV), lambda b, ci: (b, 0, 0, 0)),
        ],
        out_shape=[
            jax.ShapeDtypeStruct((batch * rows_per_batch, vw), BF16),
            jax.ShapeDtypeStruct((batch, HEADS, RET_DK, RET_DV), F32),
        ],
        scratch_shapes=[pltpu.VMEM((HEADS, RET_DK, RET_DV), F32), pltpu.VMEM((HEADS, c, c), F32)],
        compiler_params=_cparams("arbitrary", "arbitrary"),
        name="retention",
    )(log_g, rq16, rk32, rv16, gates, s0)


def _route(logits):
    gl = [logits[r:r + 1, :] for r in range(N_GROUPS)]
    gmax = functools.reduce(jnp.maximum, gl)
    g_idx = jnp.where(gl[0] == gmax, 0, jnp.where(gl[1] == gmax, 1, jnp.where(gl[2] == gmax, 2, 3)))
    g_top = 1.0 / functools.reduce(lambda a, b: a + b, [jnp.exp(x - gmax) for x in gl])
    e = []
    for r in range(EXPERTS_PER_GROUP):
        rows = [logits[N_GROUPS + g * EXPERTS_PER_GROUP + r:N_GROUPS + g * EXPERTS_PER_GROUP + r + 1, :]
                for g in range(N_GROUPS)]
        e.append(jnp.where(g_idx == 0, rows[0], jnp.where(g_idx == 1, rows[1], jnp.where(g_idx == 2, rows[2], rows[3]))))
    m1 = functools.reduce(jnp.maximum, e)
    i1 = jnp.where(e[0] == m1, 0, jnp.where(e[1] == m1, 1, jnp.where(e[2] == m1, 2, 3)))
    e2 = [jnp.where(i1 == r, -jnp.inf, e[r]) for r in range(EXPERTS_PER_GROUP)]
    m2 = functools.reduce(jnp.maximum, e2)
    i2 = jnp.where(e2[0] == m2, 0, jnp.where(e2[1] == m2, 1, jnp.where(e2[2] == m2, 2, 3)))
    p2 = jnp.exp(m2 - m1)
    w1 = g_top / (1.0 + p2)
    w2 = g_top * p2 / (1.0 + p2)
    lo = jnp.minimum(i1, i2)
    hi = jnp.maximum(i1, i2)
    w_lo = jnp.where(i1 < i2, w1, w2)
    w_hi = jnp.where(i1 < i2, w2, w1)
    pair = jnp.where(lo == 0, hi - 1, jnp.where(lo == 1, hi + 1, 5))
    return g_idx * N_PAIRS + pair, w_lo, w_hi


def _gate_merge_kernel(osp_ref, oss_ref, orp_ref, ors_ref, gsb_ref, gret_ref, wsb_ref, wret_ref, m_ref, *, n_p):
    is_p = pl.program_id(1) < n_p
    o_sb = jnp.where(is_p, osp_ref[...], oss_ref[...])
    o_ret = jnp.where(is_p, orp_ref[...], ors_ref[...])
    merged = (jax.nn.sigmoid(gsb_ref[...]) * _dot(o_sb, wsb_ref[...])
              + jax.nn.sigmoid(gret_ref[...]) * _dot(o_ret, wret_ref[...]))
    m_ref[...] = merged.astype(m_ref.dtype)


def _gate_merge(osp, oss, orp, ors, gates, wsb, wret, tm, tn):
    d = wsb.shape[1]
    tn = min(tn, d)
    n_p, n_s = osp.shape[0] // tm, oss.shape[0] // tm
    n_j = d // tn
    vw = HEADS * RET_DV
    pidx = lambda j, i: (jnp.minimum(i, n_p - 1), 0)
    sidx = lambda j, i: (jnp.clip(i - n_p, 0, n_s - 1), 0)
    return pl.pallas_call(
        functools.partial(_gate_merge_kernel, n_p=n_p),
        grid=(n_j, n_p + n_s),
        in_specs=[
            pl.BlockSpec((tm, HEADS * SB_DH), pidx),
            pl.BlockSpec((tm, HEADS * SB_DH), sidx),
            pl.BlockSpec((tm, vw), pidx),
            pl.BlockSpec((tm, vw), sidx),
            pl.BlockSpec((tm, tn), lambda j, i: (i, vw // tn + j)),
            pl.BlockSpec((tm, tn), lambda j, i: (i, (vw + d) // tn + j)),
            pl.BlockSpec((wsb.shape[0], tn), lambda j, i: (0, j)),
            pl.BlockSpec((wret.shape[0], tn), lambda j, i: (0, j)),
        ],
        out_specs=pl.BlockSpec((tm, tn), lambda j, i: (i, j)),
        out_shape=jax.ShapeDtypeStruct(((n_p + n_s) * tm, d), BF16),
        compiler_params=_cparams("arbitrary", "arbitrary"),
        name="gate_merge",
    )(osp, oss, orp, ors, gates, gates, wsb, wret)


def _out_kernel(m_ref, xp_ref, xs_ref, wout_ref, n2g_ref, wrh_ref, wrl_ref, br_ref, h_ref, route_ref, *, n_p):
    x = jnp.where(pl.program_id(0) < n_p, xp_ref[...], xs_ref[...])
    d = x.shape[1]
    h = x + _dot(m_ref[...], wout_ref[...])
    h_ref[:, :d] = h
    u2 = _rms(h, n2g_ref[...])
    u_hi = u2.astype(BF16)
    u_lo = (u2 - u_hi.astype(F32)).astype(BF16)
    logits = (_dot(u_hi, wrh_ref[...]) + (_dot(u_hi, wrl_ref[...]) + _dot(u_lo, wrh_ref[...]))) + br_ref[...]
    cls, w_lo, w_hi = _route(logits.T)
    route_ref[...] = jnp.zeros_like(route_ref)
    route_ref[0:1, :] = cls.astype(F32)
    w_rows = jnp.concatenate([w_lo, w_hi, jnp.zeros((126, w_lo.shape[1]), F32)], axis=0)
    h_ref[:, d:] = w_rows.T


def _out_proj(m, xp, xs, wout, n2g, wr_hi, wr_lo, br, tm):
    d = xp.shape[1]
    n_p, n_s = xp.shape[0] // tm, xs.shape[0] // tm
    n = n_p + n_s
    const = lambda i: (0, 0)
    return pl.pallas_call(
        functools.partial(_out_kernel, n_p=n_p),
        grid=(n,),
        in_specs=[
            pl.BlockSpec((tm, d), lambda i: (i, 0)),
            pl.BlockSpec((tm, d), lambda i: (jnp.minimum(i, n_p - 1), 0)),
            pl.BlockSpec((tm, d), lambda i: (jnp.clip(i - n_p, 0, n_s - 1), 0)),
            pl.BlockSpec(wout.shape, const, pipeline_mode=pl.Buffered(1)),
            pl.BlockSpec((1, d), const),
            pl.BlockSpec(wr_hi.shape, const),
            pl.BlockSpec(wr_lo.shape, const),
            pl.BlockSpec(br.shape, const),
        ],
        out_specs=[
            pl.BlockSpec((tm, d + 128), lambda i: (i, 0)),
            pl.BlockSpec((8, tm), lambda i: (0, i)),
        ],
        out_shape=[
            jax.ShapeDtypeStruct((n * tm, d + 128), F32),
            jax.ShapeDtypeStruct((8, n * tm), F32),
        ],
        compiler_params=_cparams("arbitrary"),
        name="out_proj",
    )(m, xp, xs, wout, n2g, wr_hi, wr_lo, br)


def _moe_kernel(elo_ref, ehi_ref, nvalid_ref, nprompt_ref, src_ref, nxt_ref, h_hbm, n2g_ref, nfg_ref,
                wg_lo, wu_lo, wd_lo, wg_hi, wu_hi, wd_hi, yp_hbm, ys_hbm, hbuf0, hbuf1, ybuf0, ybuf1, trash,
                sem_in, sem_out, *, tm, n_p):
    t = pl.program_id(0)
    n_valid = nvalid_ref[t]
    n_prompt = nprompt_ref[t]
    live = n_valid > 0
    prev_live = (t > 0) & (nvalid_ref[jnp.maximum(t - 1, 0)] > 0)
    hbuf, ybuf = (hbuf0, hbuf1), (ybuf0, ybuf1)
    d = ybuf0.shape[-1]

    def start_gather(idx_ref, s):
        for r in range(tm):
            pltpu.make_async_copy(h_hbm.at[pl.ds(idx_ref[0, 0, r], 1), :], hbuf[s].at[pl.ds(r, 1), :],
                                  sem_in.at[s]).start()

    def wait_gather(s):
        pltpu.make_async_copy(h_hbm.at[pl.ds(0, tm), :], hbuf[s], sem_in.at[s]).wait()

    def wait_scatter(s):
        pltpu.make_async_copy(ybuf[s], yp_hbm.at[pl.ds(0, tm), :], sem_out.at[s]).wait()

    @pl.when(t == 0)
    def _():
        start_gather(src_ref, 0)

    def tile(slot):
        wait_gather(slot)
        start_gather(nxt_ref, 1 - slot)
        hrows = hbuf[slot][:, :d]
        wts = hbuf[slot][:, d:]
        u = _rms(hrows, n2g_ref[...]).astype(BF16)
        y = jnp.zeros_like(hrows)
        for col, (wg, wu, wd) in enumerate(((wg_lo, wu_lo, wd_lo), (wg_hi, wu_hi, wd_hi))):
            gate = _dot(u, wg[0])
            hid = (gate * jax.nn.sigmoid(gate)) * _dot(u, wu[0])
            y = y + wts[:, col:col + 1] * _dot(hid.astype(BF16), wd[0])
        ybuf[slot][...] = _rms(hrows + y, nfg_ref[...])

        @pl.when(prev_live)
        def _():
            wait_scatter(1 - slot)

        def to_prompt(r):
            pltpu.make_async_copy(ybuf[slot].at[pl.ds(r, 1), :], yp_hbm.at[pl.ds(src_ref[0, 0, r], 1), :],
                                  sem_out.at[slot]).start()

        def to_sample(r):
            pltpu.make_async_copy(ybuf[slot].at[pl.ds(r, 1), :], ys_hbm.at[pl.ds(src_ref[0, 0, r] - n_p, 1), :],
                                  sem_out.at[slot]).start()

        def to_trash(r):
            pltpu.make_async_copy(ybuf[slot].at[pl.ds(r, 1), :], trash.at[pl.ds(r, 1), :],
                                  sem_out.at[slot]).start()

        def scatter_chunk(c, carry):
            r0 = c * SCATTER_CHUNK
            r1 = r0 + SCATTER_CHUNK
            all_prompt = r1 <= n_prompt
            all_sample = (r0 >= n_prompt) & (r1 <= n_valid)
            all_pad = r0 >= n_valid

            @pl.when(all_prompt)
            def _():
                for k in range(SCATTER_CHUNK):
                    to_prompt(r0 + k)

            @pl.when(all_sample)
            def _():
                for k in range(SCATTER_CHUNK):
                    to_sample(r0 + k)

            @pl.when(all_pad)
            def _():
                for k in range(SCATTER_CHUNK):
                    to_trash(r0 + k)

            @pl.when(jnp.logical_not(all_prompt | all_sample | all_pad))
            def _():
                for k in range(SCATTER_CHUNK):
                    r = r0 + k
                    pl.when(r < n_prompt)(functools.partial(to_prompt, r))
                    pl.when((r >= n_prompt) & (r < n_valid))(functools.partial(to_sample, r))
                    pl.when(r >= n_valid)(functools.partial(to_trash, r))

            return carry

        lax.fori_loop(0, tm // SCATTER_CHUNK, scatter_chunk, 0)

        @pl.when(nvalid_ref[t + 1] == 0)
        def _():
            wait_scatter(slot)

    for slot in (0, 1):
        parity = t % 2 == slot
        pl.when(prev_live & jnp.logical_not(live) & parity)(functools.partial(wait_gather, slot))
        pl.when(live & parity)(functools.partial(tile, slot))


def _moe(h_ext, elo, ehi, nvalid, nprompt, src, n2g, nfg, wg, wu, wd, tm, n_p):
    n, dx = h_ext.shape
    d = wg.shape[1]
    f = wg.shape[-1]
    n_t = src.shape[0]
    lo = lambda t, elo, ehi, nv, npr: (elo[t], 0, 0)
    hi = lambda t, elo, ehi, nv, npr: (ehi[t], 0, 0)
    const = lambda t, *_: (0, 0)
    grid_spec = pltpu.PrefetchScalarGridSpec(
        num_scalar_prefetch=4,
        grid=(n_t,),
        in_specs=[
            pl.BlockSpec((1, 1, tm), lambda t, *_: (t, 0, 0), memory_space=pltpu.SMEM),
            pl.BlockSpec((1, 1, tm), lambda t, *_: (jnp.minimum(t + 1, n_t - 1), 0, 0), memory_space=pltpu.SMEM),
            pl.BlockSpec(memory_space=pl.ANY),
            pl.BlockSpec((1, d), const),
            pl.BlockSpec((1, d), const),
            pl.BlockSpec((1, d, f), lo),
            pl.BlockSpec((1, d, f), lo),
            pl.BlockSpec((1, f, d), lo),
            pl.BlockSpec((1, d, f), hi),
            pl.BlockSpec((1, d, f), hi),
            pl.BlockSpec((1, f, d), hi),
        ],
        out_specs=[pl.BlockSpec(memory_space=pl.ANY), pl.BlockSpec(memory_space=pl.ANY)],
        scratch_shapes=[
            pltpu.VMEM((tm, dx), F32),
            pltpu.VMEM((tm, dx), F32),
            pltpu.VMEM((tm, d), F32),
            pltpu.VMEM((tm, d), F32),
            pltpu.VMEM((tm, d), F32),
            pltpu.SemaphoreType.DMA((2,)),
            pltpu.SemaphoreType.DMA((2,)),
        ],
    )
    return pl.pallas_call(
        functools.partial(_moe_kernel, tm=tm, n_p=n_p),
        grid_spec=grid_spec,
        out_shape=[jax.ShapeDtypeStruct((n_p, d), F32), jax.ShapeDtypeStruct((n - n_p, d), F32)],
        compiler_params=_cparams("arbitrary"),
        name="moe",
    )(elo, ehi, nvalid, nprompt, src, src, h_ext, n2g, nfg, wg, wu, wd, wg, wu, wd)


def _sort_by_class(route, tm, n_p):
    n = route.shape[1]
    n_t = n // tm + N_CLASSES + 1
    cls = route[0].astype(jnp.int32)
    onehot = (cls[:, None] == jnp.arange(N_CLASSES, dtype=jnp.int32)[None, :]).astype(jnp.int32)
    rank = jnp.sum((jnp.cumsum(onehot, axis=0) - onehot) * onehot, axis=1)
    tiles_per = (jnp.sum(onehot, axis=0) + tm - 1) // tm
    tile_end = jnp.cumsum(tiles_per)
    nused = tile_end[-1]
    pos = (tile_end - tiles_per)[cls] * tm + rank
    tile_ids = jnp.minimum(jnp.arange(n_t, dtype=jnp.int32), nused - 1)
    tile_cls = jnp.minimum(jnp.sum((tile_ids[:, None] >= tile_end[None, :]).astype(jnp.int32), axis=1), N_CLASSES - 1)
    pair_lo = jnp.array([0, 0, 0, 1, 1, 2], jnp.int32)
    pair_hi = jnp.array([1, 2, 3, 2, 3, 3], jnp.int32)
    elo = (tile_cls // N_PAIRS) * EXPERTS_PER_GROUP + pair_lo[tile_cls % N_PAIRS]
    ehi = (tile_cls // N_PAIRS) * EXPERTS_PER_GROUP + pair_hi[tile_cls % N_PAIRS]
    p = n_t * tm
    src = jnp.full((p,), -1, jnp.int32).at[pos].set(jnp.arange(n, dtype=jnp.int32))
    nvalid = jnp.sum((src >= 0).reshape(n_t, tm), axis=1).astype(jnp.int32)
    nprompt = jnp.sum(((src >= 0) & (src < n_p)).reshape(n_t, tm), axis=1).astype(jnp.int32)
    src = jnp.maximum(src, 0)
    return elo, ehi, nvalid, nprompt, src.reshape(n_t, 1, tm)


def kernel(x_prompt, x_sample, cache_sb_k, cache_sb_v, state_ret, meta, norm1_g, w_in, w_sb_o, w_ret_o,
           w_out, norm2_g, w_grp, b_grp, w_exp, b_exp, w_gate, w_up, w_down, normf_g):
    batch, seq, d = x_prompt.shape
    dec_b, dec_n, _ = x_sample.shape
    depth, _, past = cache_sb_k.shape[:3]
    assert depth == 1
    n_p, n_s = batch * seq, dec_b * dec_n
    sbw, qkw, vw = HEADS * SB_DH, HEADS * RET_DK, HEADS * RET_DV
    tm = min(ROW_TILE, n_s)

    xp = x_prompt.reshape(n_p, d)
    xs = x_sample.reshape(n_s, d)
    u = _norm_all(xp, xs, meta, norm1_g, tm)
    m_rows = u.shape[0]
    meta_row0 = n_p + n_s

    pos = jnp.concatenate([
        jnp.tile(jnp.arange(seq, dtype=F32), batch),
        jnp.tile(past + jnp.arange(dec_n, dtype=F32), dec_b),
        jnp.arange(m_rows - meta_row0, dtype=F32) - N_META,
    ])
    inv_freq = 1.0 / (ROPE_BASE ** (jnp.arange(0, RET_DK, 2, dtype=F32) / RET_DK))
    ang = pos[:, None] * inv_freq[None, :]
    cos_t = jnp.concatenate([jnp.cos(ang), jnp.cos(ang)], axis=1)
    sin_t = jnp.concatenate([-jnp.sin(ang), jnp.sin(ang)], axis=1)

    proj = functools.partial(_project, u, w_in[0], tm=tm, tn=COL_TILE)
    c = 0
    sb_q16 = proj(c, sbw, "bf16"); c += sbw
    sb_k16, k_prompt, k_sample = _project_kv(u, w_in[0], c, tm, batch, seq, n_s); c += sbw
    sb_v16, v_prompt, v_sample = _project_kv(u, w_in[0], c, tm, batch, seq, n_s); c += sbw
    r_q16 = proj(c, qkw, "rot", rot=(cos_t, sin_t), out_dtype=BF16); c += qkw
    r_k32 = proj(c, qkw, "rot", rot=(cos_t, sin_t), scale=RET_DK ** -0.5, out_dtype=F32); c += qkw
    r_v16 = proj(c, vw, "bf16"); c += vw
    gates = proj(c, vw + 2 * d, "f32")

    o_sb_p = _sb_prompt(sb_q16, sb_k16, sb_v16, batch, seq, meta_row0, min(SB_TILE, seq))
    o_sb_s = _sb_sample(sb_q16, sb_k16, sb_v16, cache_sb_k, cache_sb_v, dec_n, n_p, SB_TILE)

    log_g = jnp.log1p(-jnp.power(2.0, -5.0 - jnp.arange(HEADS, dtype=F32)))
    ret = functools.partial(_retention, log_g, r_q16, r_k32, r_v16, gates)
    zero_state = jnp.zeros((1, HEADS, RET_DK, RET_DV), F32)
    _, s_meta = ret(zero_state, 1, N_META, meta_row0, N_META, True)
    o_ret_p, s_p = ret(s_meta, batch, seq, 0, min(RET_CHUNK, seq), True)
    o_ret_s, s_s = ret(state_ret[0], dec_b, dec_n, n_p, dec_n, False)

    n_route = N_GROUPS + N_GROUPS * EXPERTS_PER_GROUP
    w_router = jnp.pad(jnp.concatenate([w_grp[0], w_exp[0]], axis=1), ((0, 0), (0, 128 - n_route)))
    b_router = jnp.pad(jnp.concatenate([b_grp[0], b_exp[0]]), (0, 128 - n_route)).reshape(1, 128)
    tmm = min(MERGE_TILE, n_s)
    merged = _gate_merge(o_sb_p, o_sb_s, o_ret_p, o_ret_s, gates,
                         w_sb_o[0].astype(BF16), w_ret_o[0].astype(BF16), tmm, COL_TILE)
    wr_hi = w_router.astype(BF16)
    wr_lo = (w_router - wr_hi.astype(F32)).astype(BF16)
    h, route = _out_proj(merged, xp, xs, w_out[0].astype(BF16), norm2_g, wr_hi, wr_lo, b_router, tmm)

    tmo = min(MOE_TILE, n_s)
    elo, ehi, nvalid, nprompt, src = _sort_by_class(route, tmo, n_p)
    y_p, y_s = _moe(h, elo, ehi, nvalid, nprompt, src, norm2_g, normf_g.reshape(1, d),
                    w_gate[0].astype(BF16), w_up[0].astype(BF16), w_down[0].astype(BF16), tmo, n_p)
    y_prompt = y_p.reshape(batch, seq, d)
    y_sample = y_s.reshape(dec_b, dec_n, d)

    pshape = (1, batch, N_META + seq, HEADS, SB_DH)
    sshape = (1, dec_b, dec_n, HEADS, SB_DH)
    return (y_prompt, y_sample, k_prompt.reshape(pshape), v_prompt.reshape(pshape), s_p[None],
            k_sample.reshape(sshape), v_sample.reshape(sshape), s_s[None])
```

```python
import functools

import jax
import jax.numpy as jnp
from jax import lax
from jax.experimental import pallas as pl
from jax.experimental.pallas import tpu as pltpu

F32 = jnp.float32
BF16 = jnp.bfloat16

N_META = 16
HEADS = 8
SB_DH = 128
RET_DK = 128
RET_DV = 256
N_GROUPS = 4
EXPERTS_PER_GROUP = 4
N_PAIRS = 6
N_CLASSES = N_GROUPS * N_PAIRS
ROPE_BASE = 10000.0
EPS = 1e-6
SB_LOG_CUTOFF = -88.0

ROW_TILE = 512
COL_TILE = 1024
MERGE_TILE = 512
MOE_TILE = 256
SCATTER_CHUNK = 8
SB_TILE = 128
RET_CHUNK = 256
VMEM_LIMIT = 56 * 1024 * 1024


def _cparams(*sem):
    return pltpu.CompilerParams(dimension_semantics=sem, vmem_limit_bytes=VMEM_LIMIT)


def _rms(x, g):
    return (x * lax.rsqrt(jnp.mean(x * x, axis=-1, keepdims=True) + EPS)) * g


def _dot(a, b):
    return jnp.dot(a, b, preferred_element_type=F32)


def _dot_nt(a, b):
    return lax.dot_general(a, b, (((1,), (1,)), ((), ())), preferred_element_type=F32)


def _dot_tn(a, b):
    return lax.dot_general(a, b, (((0,), (0,)), ((), ())), preferred_element_type=F32)


def _norm_proj_kernel(xp_ref, xs_ref, meta_ref, g_ref, w_ref, u_ref, o_ref, w16, *, n_p, n_s):
    i = pl.program_id(0)

    @pl.when(i == 0)
    def _():
        w16[...] = w_ref[...].astype(BF16)

    @pl.when(i < n_p)
    def _():
        u_ref[...] = _rms(xp_ref[...], g_ref[...]).astype(BF16)

    @pl.when((i >= n_p) & (i < n_p + n_s))
    def _():
        u_ref[...] = _rms(xs_ref[...], g_ref[...]).astype(BF16)

    @pl.when(i >= n_p + n_s)
    def _():
        u_ref[...] = jnp.zeros_like(u_ref)
        u_ref[0:N_META, :] = _rms(meta_ref[...], g_ref[...]).astype(BF16)

    o_ref[...] = _dot(u_ref[...], w16[...]).astype(BF16)


def _norm_and_project(xp, xs, meta, g, w, width, tm):
    d = xp.shape[1]
    n_p, n_s = xp.shape[0] // tm, xs.shape[0] // tm
    m = (n_p + n_s + 1) * tm
    return pl.pallas_call(
        functools.partial(_norm_proj_kernel, n_p=n_p, n_s=n_s),
        grid=(n_p + n_s + 1,),
        in_specs=[
            pl.BlockSpec((tm, d), lambda i: (jnp.minimum(i, n_p - 1), 0)),
            pl.BlockSpec((tm, d), lambda i: (jnp.clip(i - n_p, 0, n_s - 1), 0)),
            pl.BlockSpec((N_META, d), lambda i: (0, 0)),
            pl.BlockSpec((1, d), lambda i: (0, 0)),
            pl.BlockSpec((d, width), lambda i: (0, 0), pipeline_mode=pl.Buffered(1)),
        ],
        out_specs=[pl.BlockSpec((tm, d), lambda i: (i, 0)), pl.BlockSpec((tm, width), lambda i: (i, 0))],
        out_shape=[jax.ShapeDtypeStruct((m, d), BF16), jax.ShapeDtypeStruct((m, width), BF16)],
        scratch_shapes=[pltpu.VMEM((d, width), BF16)],
        compiler_params=_cparams("arbitrary"),
        name="norm_proj",
    )(xp, xs, meta, g, w)


def _proj_kernel(u_ref, w_ref, *refs, mode, scale):
    w16 = refs[-1]

    @pl.when(pl.program_id(1) == 0)
    def _():
        w16[...] = w_ref[...].astype(BF16)

    acc = _dot(u_ref[...], w16[...])
    if mode == "f32":
        refs[0][...] = acc
    elif mode == "bf16":
        refs[0][...] = acc.astype(BF16)
    else:
        cos_ref, sin_ref, o_ref = refs[:3]
        cos, sin = cos_ref[...], sin_ref[...]
        for hh in range(acc.shape[1] // RET_DK):
            x = acc[:, hh * RET_DK:(hh + 1) * RET_DK]
            r = x * cos + pltpu.roll(x, RET_DK // 2, 1) * sin
            if scale is not None:
                r = r * scale
            o_ref[:, hh * RET_DK:(hh + 1) * RET_DK] = r.astype(o_ref.dtype)


def _project(u, w, col0, width, mode, tm, tn, rot=None, scale=None, out_dtype=None):
    m, d = u.shape
    tn = min(tn, width)
    n_j, n_i = width // tn, m // tm
    j0 = col0 // tn
    in_specs = [
        pl.BlockSpec((tm, d), lambda j, i: (i, 0)),
        pl.BlockSpec((d, tn), lambda j, i: (0, j0 + j)),
    ]
    args = [u, w]
    out_specs = pl.BlockSpec((tm, tn), lambda j, i: (i, j))
    if mode == "rot":
        cos_t, sin_t, table_block = rot
        in_specs += [pl.BlockSpec((tm, RET_DK), lambda j, i: (table_block(i), 0))] * 2
        args += [cos_t, sin_t]
        out_shape = jax.ShapeDtypeStruct((m, width), out_dtype)
    else:
        out_shape = jax.ShapeDtypeStruct((m, width), F32 if mode == "f32" else BF16)
    return pl.pallas_call(
        functools.partial(_proj_kernel, mode=mode, scale=scale),
        grid=(n_j, n_i),
        in_specs=in_specs,
        out_specs=out_specs,
        out_shape=out_shape,
        scratch_shapes=[pltpu.VMEM((d, tn), BF16)],
        compiler_params=_cparams("arbitrary", "arbitrary"),
        name="proj_" + mode,
    )(*args)


def _proj_kv_kernel(u_ref, w_ref, o16_ref, outp_hbm, outs_hbm, w16, stage, sem,
                    *, tm, n_pt, n_st, tiles_per_batch, batch, rows_per_batch):
    i = pl.program_id(0)

    @pl.when(i == 0)
    def _():
        w16[...] = w_ref[...].astype(BF16)

    acc = _dot(u_ref[...], w16[...])
    o16_ref[...] = acc.astype(BF16)

    def tile_copy(dst):
        return pltpu.make_async_copy(stage, dst, sem)

    @pl.when(i > 0)
    def _():
        tile_copy(outp_hbm.at[pl.ds(0, tm)]).wait()

    for hd in range(HEADS):
        stage[:, hd, :] = acc[:, hd * SB_DH:(hd + 1) * SB_DH]

    @pl.when(i < n_pt)
    def _():
        b = i // tiles_per_batch
        r0 = (i % tiles_per_batch) * tm
        tile_copy(outp_hbm.at[pl.ds(b * rows_per_batch + N_META + r0, tm)]).start()

    @pl.when((i >= n_pt) & (i < n_pt + n_st))
    def _():
        tile_copy(outs_hbm.at[pl.ds((i - n_pt) * tm, tm)]).start()

    @pl.when(i == n_pt + n_st)
    def _():
        copies = [pltpu.make_async_copy(stage.at[pl.ds(0, N_META)],
                                        outp_hbm.at[pl.ds(b * rows_per_batch, N_META)], sem)
                  for b in range(batch)]
        for c in copies:
            c.start()
        for c in copies:
            c.wait()


def _project_kv(u, w, col0, tm, batch, seq, n_s):
    m, d = u.shape
    width = HEADS * SB_DH
    n_pt, n_st = batch * seq // tm, n_s // tm
    assert seq % tm == 0 and m // tm == n_pt + n_st + 1
    j0 = col0 // width
    rows_per_batch = N_META + seq
    return pl.pallas_call(
        functools.partial(_proj_kv_kernel, tm=tm, n_pt=n_pt, n_st=n_st, tiles_per_batch=seq // tm,
                          batch=batch, rows_per_batch=rows_per_batch),
        grid=(m // tm,),
        in_specs=[
            pl.BlockSpec((tm, d), lambda i: (i, 0)),
            pl.BlockSpec((d, width), lambda i: (0, j0), pipeline_mode=pl.Buffered(1)),
        ],
        out_specs=[
            pl.BlockSpec((tm, width), lambda i: (i, 0)),
            pl.BlockSpec(memory_space=pl.ANY),
            pl.BlockSpec(memory_space=pl.ANY),
        ],
        out_shape=[
            jax.ShapeDtypeStruct((m, width), BF16),
            jax.ShapeDtypeStruct((batch * rows_per_batch, HEADS, SB_DH), F32),
            jax.ShapeDtypeStruct((n_s, HEADS, SB_DH), F32),
        ],
        scratch_shapes=[
            pltpu.VMEM((d, width), BF16),
            pltpu.VMEM((tm, HEADS, SB_DH), F32),
            pltpu.SemaphoreType.DMA(()),
        ],
        compiler_params=_cparams("arbitrary"),
        name="proj_kv",
    )(u, w)


def _upper_ones(n):
    r = lax.broadcasted_iota(jnp.int32, (n, n), 0)
    c = lax.broadcasted_iota(jnp.int32, (n, n), 1)
    return (r > c).astype(BF16)


def _sb_tiles(tiles, tri):
    n = range(len(tiles))
    q, k, v, carry, acc, valid = zip(*tiles)
    z = [_dot_nt(q[i], k[i]) * (SB_DH ** -0.5) for i in n]
    e = [jnp.exp(-jnp.abs(z[i])) for i in n]
    sp = [jnp.maximum(z[i], 0.0) + jnp.log(1.0 + e[i]) for i in n]
    lf = [-sp[i] if valid[i] is None else jnp.where(valid[i], -sp[i], 0.0) for i in n]
    hi = [lf[i].astype(BF16) for i in n]
    lo = [(lf[i] - hi[i].astype(F32)).astype(BF16) for i in n]
    later = [_dot(hi[i], tri) + _dot(lo[i], tri) for i in n]
    a = [jnp.exp((z[i] - sp[i]) + (later[i] + carry[i])) for i in n]
    a = [a[i] if valid[i] is None else jnp.where(valid[i], a[i], 0.0) for i in n]
    acc = [acc[i] + _dot(a[i].astype(BF16), v[i]) for i in n]
    carry = [carry[i] + jnp.sum(lf[i], axis=-1, keepdims=True) for i in n]
    return list(zip(carry, acc))


def _sb_tile(q, k, v, carry, acc, tri, valid=None):
    return _sb_tiles([(q, k, v, carry, acc, valid)], tri)[0]


def _sb_prompt_kernel(q_ref, k_ref, v_ref, mk_ref, mv_ref, o_ref, *, t, n_q, unroll):
    tri = _upper_ones(t)
    tri2 = _upper_ones(2 * t)
    tri_m = _upper_ones(N_META)
    col_minus_row = (lax.broadcasted_iota(jnp.int32, (t, 2 * t), 1)
                     - lax.broadcasted_iota(jnp.int32, (t, 2 * t), 0))

    def window(i):
        r0 = pl.multiple_of(i * t, t)
        w0 = pl.multiple_of(jnp.maximum(i - 1, 0) * t, t)
        valid = col_minus_row < (r0 - w0)
        return (q_ref[pl.ds(r0, t), :], k_ref[pl.ds(w0, 2 * t), :], v_ref[pl.ds(w0, 2 * t), :],
                jnp.zeros((t, 1), F32), jnp.zeros((t, SB_DH), F32), valid)

    def older_keys(i, q, carry, acc):
        def cond(s):
            return (s[0] >= 0) & (jnp.max(s[1]) > SB_LOG_CUTOFF)

        def body(s):
            j, carry, acc = s
            c0 = pl.multiple_of(j * t, t)
            carry, acc = _sb_tile(q, k_ref[pl.ds(c0, t), :], v_ref[pl.ds(c0, t), :], carry, acc, tri)
            return j - 1, carry, acc

        _, carry, acc = lax.while_loop(cond, body, (i - 2, carry, acc))
        return lax.cond(
            jnp.max(carry) > SB_LOG_CUTOFF,
            lambda: _sb_tile(q, mk_ref[...], mv_ref[...], carry, acc, tri_m)[1],
            lambda: acc,
        )

    def group(p, _):
        tiles = [window(p * unroll + u) for u in range(unroll)]
        done = _sb_tiles(tiles, tri2)
        for u, (carry, acc) in enumerate(done):
            i = p * unroll + u
            q = tiles[u][0]
            acc = lax.cond(
                jnp.max(carry) > SB_LOG_CUTOFF,
                functools.partial(older_keys, i, q, carry, acc),
                lambda acc=acc: acc,
            )
            o_ref[pl.ds(pl.multiple_of(i * t, t), t), :] = acc.astype(o_ref.dtype)
        return 0

    lax.fori_loop(0, n_q // unroll, group, 0)


def _sb_prompt(q16, k16, v16, batch, seq, meta_row0, t):
    n_q = seq // t
    assert n_q >= 2
    unroll = max(u for u in (1, 2, 4, 8) if n_q % u == 0)
    mb = meta_row0 // N_META
    blk = lambda b, h: (b, h)
    mblk = lambda b, h: (mb, h)
    return pl.pallas_call(
        functools.partial(_sb_prompt_kernel, t=t, n_q=n_q, unroll=unroll),
        grid=(batch, HEADS),
        in_specs=[
            pl.BlockSpec((seq, SB_DH), blk),
            pl.BlockSpec((seq, SB_DH), blk),
            pl.BlockSpec((seq, SB_DH), blk),
            pl.BlockSpec((N_META, SB_DH), mblk),
            pl.BlockSpec((N_META, SB_DH), mblk),
        ],
        out_specs=pl.BlockSpec((seq, SB_DH), blk),
        out_shape=jax.ShapeDtypeStruct((batch * seq, HEADS * SB_DH), BF16),
        compiler_params=_cparams("arbitrary", "arbitrary"),
        name="sb_prompt",
    )(q16, k16, v16, k16, v16)


def _sb_sample_kernel(q_ref, k_ref, v_ref, ckl_ref, cvl_ref, ck_hbm, cv_hbm, o_ref, kbuf, vbuf, sem,
                      *, n, t, n_past):
    b = pl.program_id(0)
    tri_n = _upper_ones(n)
    tri = _upper_ones(t)
    causal = lax.broadcasted_iota(jnp.int32, (n, n), 1) < lax.broadcasted_iota(jnp.int32, (n, n), 0)

    def older_keys(hd, q, carry, acc):
        def cond(s):
            return (s[0] >= 0) & (jnp.max(s[1]) > SB_LOG_CUTOFF)

        def body(s):
            j, carry, acc = s
            rows = pl.ds(pl.multiple_of(j * t, t), t)
            ck = pltpu.make_async_copy(ck_hbm.at[0, b, rows], kbuf, sem.at[0])
            cv = pltpu.make_async_copy(cv_hbm.at[0, b, rows], vbuf, sem.at[1])
            ck.start()
            cv.start()
            ck.wait()
            cv.wait()
            carry, acc = _sb_tile(q, kbuf[:, hd, :].astype(BF16), vbuf[:, hd, :].astype(BF16), carry, acc, tri)
            return j - 1, carry, acc

        return lax.while_loop(cond, body, (n_past - 2, carry, acc))[2]

    cols = [slice(hd * SB_DH, (hd + 1) * SB_DH) for hd in range(HEADS)]
    qs = [q_ref[:, c] for c in cols]
    new = _sb_tiles([(qs[hd], k_ref[:, cols[hd]], v_ref[:, cols[hd]], jnp.zeros((n, 1), F32),
                      jnp.zeros((n, SB_DH), F32), causal) for hd in range(HEADS)], tri_n)
    old = _sb_tiles([(qs[hd], ckl_ref[0, 0, :, hd, :].astype(BF16), cvl_ref[0, 0, :, hd, :].astype(BF16),
                      new[hd][0], new[hd][1], None) for hd in range(HEADS)], tri)
    for hd, (carry, acc) in enumerate(old):
        q = qs[hd]
        acc = lax.cond(
            jnp.max(carry) > SB_LOG_CUTOFF,
            functools.partial(older_keys, hd, q, carry, acc),
            lambda acc=acc: acc,
        )
        o_ref[:, hd * SB_DH:(hd + 1) * SB_DH] = acc.astype(o_ref.dtype)


def _sb_sample(q16, k16, v16, cache_k, cache_v, n, row0, t):
    _, dec_b, past = cache_k.shape[:3]
    t = min(t, past)
    n_past = past // t
    rb = row0 // n
    w = HEADS * SB_DH
    blk = lambda b: (rb + b, 0)
    last = lambda b: (0, b, n_past - 1, 0, 0)
    return pl.pallas_call(
        functools.partial(_sb_sample_kernel, n=n, t=t, n_past=n_past),
        grid=(dec_b,),
        in_specs=[
            pl.BlockSpec((n, w), blk),
            pl.BlockSpec((n, w), blk),
            pl.BlockSpec((n, w), blk),
            pl.BlockSpec((1, 1, t, HEADS, SB_DH), last),
            pl.BlockSpec((1, 1, t, HEADS, SB_DH), last),
            pl.BlockSpec(memory_space=pl.ANY),
            pl.BlockSpec(memory_space=pl.ANY),
        ],
        out_specs=pl.BlockSpec((n, w), lambda b: (b, 0)),
        out_shape=jax.ShapeDtypeStruct((dec_b * n, w), BF16),
        scratch_shapes=[
            pltpu.VMEM((t, HEADS, SB_DH), F32),
            pltpu.VMEM((t, HEADS, SB_DH), F32),
            pltpu.SemaphoreType.DMA((2,)),
        ],
        compiler_params=_cparams("arbitrary"),
        name="sb_sample",
    )(q16, k16, v16, cache_k, cache_v, cache_k, cache_v)


def _ret_kernel(lg_ref, q_ref, k_ref, v_ref, g_ref, s0_ref, o_ref, s_out_ref, s_scr, d_scr, *, c, n_c):
    ci = pl.program_id(1)

    @pl.when((pl.program_id(0) == 0) & (ci == 0))
    def _():
        rel = (lax.broadcasted_iota(jnp.int32, (c, c), 0) - lax.broadcasted_iota(jnp.int32, (c, c), 1)).astype(F32)
        for hd in range(HEADS):
            d_scr[hd] = jnp.where(rel >= 0, jnp.exp(jnp.maximum(rel, 0.0) * lg_ref[hd]), 0.0)

    @pl.when(ci == 0)
    def _():
        s_scr[...] = s0_ref[0]

    tcol = lax.broadcasted_iota(jnp.int32, (c, 1), 0).astype(F32)
    for hd in range(HEADS):
        lg = lg_ref[hd]
        qk = slice(hd * RET_DK, (hd + 1) * RET_DK)
        vv = slice(hd * RET_DV, (hd + 1) * RET_DV)
        q = q_ref[:, qk]
        kf = k_ref[:, qk]
        v = v_ref[:, vv]
        s = s_scr[hd]
        scores = _dot_nt(q, kf.astype(BF16)) * d_scr[hd]
        o = _dot(scores.astype(BF16), v) + _dot(q, s.astype(BF16)) * jnp.exp((tcol + 1.0) * lg)
        k_dec = (kf * jnp.exp((c - 1.0 - tcol) * lg)).astype(BF16)
        s_scr[hd] = jnp.exp(c * lg) * s + _dot_tn(k_dec, v)
        o = o * lax.rsqrt(jnp.mean(o * o, axis=-1, keepdims=True) + EPS)
        g = g_ref[:, vv]
        o_ref[:, vv] = (o * (g * jax.nn.sigmoid(g))).astype(o_ref.dtype)

    @pl.when(ci == n_c - 1)
    def _():
        s_out_ref[0] = s_scr[...]


def _retention(log_g, rq16, rk32, rv16, gates, s0, batch, rows_per_batch, row0, c, shared_s0):
    n_c = rows_per_batch // c
    rb = row0 // c
    qkw, vw = HEADS * RET_DK, HEADS * RET_DV
    row = lambda b, ci: (rb + b * n_c + ci, 0)
    sidx = (lambda b, ci: (0, 0, 0, 0)) if shared_s0 else (lambda b, ci: (b, 0, 0, 0))
    return pl.pallas_call(
        functools.partial(_ret_kernel, c=c, n_c=n_c),
        grid=(batch, n_c),
        in_specs=[
            pl.BlockSpec(memory_space=pltpu.SMEM),
            pl.BlockSpec((c, qkw), row),
            pl.BlockSpec((c, qkw), row),
            pl.BlockSpec((c, vw), row),
            pl.BlockSpec((c, vw), row),
            pl.BlockSpec((1, HEADS, RET_DK, RET_DV), sidx),
        ],
        out_specs=[
            pl.BlockSpec((c, vw), lambda b, ci: (b * n_c + ci, 0)),
            pl.BlockSpec((1, HEADS, RET_DK, RET_DV), lambda b, ci: (b, 0, 0, 0)),
        ],
        out_shape=[
            jax.ShapeDtypeStruct((batch * rows_per_batch, vw), BF16),
            jax.ShapeDtypeStruct((batch, HEADS, RET_DK, RET_DV), F32),
        ],
        scratch_shapes=[pltpu.VMEM((HEADS, RET_DK, RET_DV), F32), pltpu.VMEM((HEADS, c, c), F32)],
        compiler_params=_cparams("arbitrary", "arbitrary"),
        name="retention",
    )(log_g, rq16, rk32, rv16, gates, s0)


def _route(logits):
    gl = [logits[r:r + 1, :] for r in range(N_GROUPS)]
    gmax = functools.reduce(jnp.maximum, gl)
    g_idx = jnp.where(gl[0] == gmax, 0, jnp.where(gl[1] == gmax, 1, jnp.where(gl[2] == gmax, 2, 3)))
    g_top = 1.0 / functools.reduce(lambda a, b: a + b, [jnp.exp(x - gmax) for x in gl])
    e = []
    for r in range(EXPERTS_PER_GROUP):
        rows = [logits[N_GROUPS + g * EXPERTS_PER_GROUP + r:N_GROUPS + g * EXPERTS_PER_GROUP + r + 1, :]
                for g in range(N_GROUPS)]
        e.append(jnp.where(g_idx == 0, rows[0], jnp.where(g_idx == 1, rows[1], jnp.where(g_idx == 2, rows[2], rows[3]))))
    m1 = functools.reduce(jnp.maximum, e)
    i1 = jnp.where(e[0] == m1, 0, jnp.where(e[1] == m1, 1, jnp.where(e[2] == m1, 2, 3)))
    e2 = [jnp.where(i1 == r, -jnp.inf, e[r]) for r in range(EXPERTS_PER_GROUP)]
    m2 = functools.reduce(jnp.maximum, e2)
    i2 = jnp.where(e2[0] == m2, 0, jnp.where(e2[1] == m2, 1, jnp.where(e2[2] == m2, 2, 3)))
    p2 = jnp.exp(m2 - m1)
    w1 = g_top / (1.0 + p2)
    w2 = g_top * p2 / (1.0 + p2)
    lo = jnp.minimum(i1, i2)
    hi = jnp.maximum(i1, i2)
    w_lo = jnp.where(i1 < i2, w1, w2)
    w_hi = jnp.where(i1 < i2, w2, w1)
    pair = jnp.where(lo == 0, hi - 1, jnp.where(lo == 1, hi + 1, 5))
    return g_idx * N_PAIRS + pair, w_lo, w_hi


def _gate_merge_kernel(osp_ref, oss_ref, orp_ref, ors_ref, gsb_ref, gret_ref, wsb_ref, wret_ref, m_ref, *, n_p):
    is_p = pl.program_id(1) < n_p
    o_sb = jnp.where(is_p, osp_ref[...], oss_ref[...])
    o_ret = jnp.where(is_p, orp_ref[...], ors_ref[...])
    merged = (jax.nn.sigmoid(gsb_ref[...]) * _dot(o_sb, wsb_ref[...])
              + jax.nn.sigmoid(gret_ref[...]) * _dot(o_ret, wret_ref[...]))
    m_ref[...] = merged.astype(m_ref.dtype)


def _gate_merge(osp, oss, orp, ors, gates, wsb, wret, tm, tn):
    d = wsb.shape[1]
    tn = min(tn, d)
    n_p, n_s = osp.shape[0] // tm, oss.shape[0] // tm
    n_j = d // tn
    vw = HEADS * RET_DV
    pidx = lambda j, i: (jnp.minimum(i, n_p - 1), 0)
    sidx = lambda j, i: (jnp.clip(i - n_p, 0, n_s - 1), 0)
    return pl.pallas_call(
        functools.partial(_gate_merge_kernel, n_p=n_p),
        grid=(n_j, n_p + n_s),
        in_specs=[
            pl.BlockSpec((tm, HEADS * SB_DH), pidx),
            pl.BlockSpec((tm, HEADS * SB_DH), sidx),
            pl.BlockSpec((tm, vw), pidx),
            pl.BlockSpec((tm, vw), sidx),
            pl.BlockSpec((tm, tn), lambda j, i: (i, vw // tn + j)),
            pl.BlockSpec((tm, tn), lambda j, i: (i, (vw + d) // tn + j)),
            pl.BlockSpec((wsb.shape[0], tn), lambda j, i: (0, j)),
            pl.BlockSpec((wret.shape[0], tn), lambda j, i: (0, j)),
        ],
        out_specs=pl.BlockSpec((tm, tn), lambda j, i: (i, j)),
        out_shape=jax.ShapeDtypeStruct(((n_p + n_s) * tm, d), BF16),
        compiler_params=_cparams("arbitrary", "arbitrary"),
        name="gate_merge",
    )(osp, oss, orp, ors, gates, gates, wsb, wret)


def _out_kernel(m_ref, xp_ref, xs_ref, wout_ref, n2g_ref, wrc_ref, br_ref, h_ref, route_ref, *, n_p):
    x = jnp.where(pl.program_id(0) < n_p, xp_ref[...], xs_ref[...])
    d = x.shape[1]
    h = x + _dot(m_ref[...], wout_ref[...])
    h_ref[:, :d] = h
    u2 = _rms(h, n2g_ref[...])
    u_hi = u2.astype(BF16)
    u_lo = (u2 - u_hi.astype(F32)).astype(BF16)
    hi_both = _dot(u_hi, wrc_ref[...])
    logits = (hi_both[:, :128] + (hi_both[:, 128:] + _dot(u_lo, wrc_ref[:, :128]))) + br_ref[...]
    cls, w_lo, w_hi = _route(logits.T)
    route_ref[...] = jnp.zeros_like(route_ref)
    route_ref[0:1, :] = cls.astype(F32)
    w_rows = jnp.concatenate([w_lo, w_hi, jnp.zeros((126, w_lo.shape[1]), F32)], axis=0)
    h_ref[:, d:] = w_rows.T


def _out_proj(m, xp, xs, wout, n2g, wr_cat, br, tm):
    d = xp.shape[1]
    n_p, n_s = xp.shape[0] // tm, xs.shape[0] // tm
    n = n_p + n_s
    const = lambda i: (0, 0)
    return pl.pallas_call(
        functools.partial(_out_kernel, n_p=n_p),
        grid=(n,),
        in_specs=[
            pl.BlockSpec((tm, d), lambda i: (i, 0)),
            pl.BlockSpec((tm, d), lambda i: (jnp.minimum(i, n_p - 1), 0)),
            pl.BlockSpec((tm, d), lambda i: (jnp.clip(i - n_p, 0, n_s - 1), 0)),
            pl.BlockSpec(wout.shape, const, pipeline_mode=pl.Buffered(1)),
            pl.BlockSpec((1, d), const),
            pl.BlockSpec(wr_cat.shape, const),
            pl.BlockSpec(br.shape, const),
        ],
        out_specs=[
            pl.BlockSpec((tm, d + 128), lambda i: (i, 0)),
            pl.BlockSpec((8, tm), lambda i: (0, i)),
        ],
        out_shape=[
            jax.ShapeDtypeStruct((n * tm, d + 128), F32),
            jax.ShapeDtypeStruct((8, n * tm), F32),
        ],
        compiler_params=_cparams("arbitrary"),
        name="out_proj",
    )(m, xp, xs, wout, n2g, wr_cat, br)


def _moe_kernel(elo_ref, ehi_ref, nvalid_ref, nprompt_ref, src_ref, nxt_ref, h_hbm, n2g_ref, nfg_ref,
                wg_lo, wu_lo, wd_lo, wg_hi, wu_hi, wd_hi, yp_hbm, ys_hbm, hbuf0, hbuf1, ybuf0, ybuf1, trash,
                sem_in, sem_out, *, tm, n_p):
    t = pl.program_id(0)
    n_valid = nvalid_ref[t]
    n_prompt = nprompt_ref[t]
    live = n_valid > 0
    prev_live = (t > 0) & (nvalid_ref[jnp.maximum(t - 1, 0)] > 0)
    hbuf, ybuf = (hbuf0, hbuf1), (ybuf0, ybuf1)
    d = ybuf0.shape[-1]

    def start_gather(idx_ref, s):
        for r in range(tm):
            pltpu.make_async_copy(h_hbm.at[pl.ds(idx_ref[0, 0, r], 1), :], hbuf[s].at[pl.ds(r, 1), :],
                                  sem_in.at[s]).start()

    def wait_gather(s):
        pltpu.make_async_copy(h_hbm.at[pl.ds(0, tm), :], hbuf[s], sem_in.at[s]).wait()

    def wait_scatter(s):
        pltpu.make_async_copy(ybuf[s], yp_hbm.at[pl.ds(0, tm), :], sem_out.at[s]).wait()

    @pl.when(t == 0)
    def _():
        start_gather(src_ref, 0)

    def tile(slot):
        wait_gather(slot)
        start_gather(nxt_ref, 1 - slot)
        hrows = hbuf[slot][:, :d]
        wts = hbuf[slot][:, d:]
        u = _rms(hrows, n2g_ref[...]).astype(BF16)
        y = jnp.zeros_like(hrows)
        for col, (wg, wu, wd) in enumerate(((wg_lo, wu_lo, wd_lo), (wg_hi, wu_hi, wd_hi))):
            gate = _dot(u, wg[0])
            hid = (gate * jax.nn.sigmoid(gate)) * _dot(u, wu[0])
            y = y + wts[:, col:col + 1] * _dot(hid.astype(BF16), wd[0])
        ybuf[slot][...] = _rms(hrows + y, nfg_ref[...])

        @pl.when(prev_live)
        def _():
            wait_scatter(1 - slot)

        def to_prompt(r):
            pltpu.make_async_copy(ybuf[slot].at[pl.ds(r, 1), :], yp_hbm.at[pl.ds(src_ref[0, 0, r], 1), :],
                                  sem_out.at[slot]).start()

        def to_sample(r):
            pltpu.make_async_copy(ybuf[slot].at[pl.ds(r, 1), :], ys_hbm.at[pl.ds(src_ref[0, 0, r] - n_p, 1), :],
                                  sem_out.at[slot]).start()

        def to_trash(r):
            pltpu.make_async_copy(ybuf[slot].at[pl.ds(r, 1), :], trash.at[pl.ds(r, 1), :],
                                  sem_out.at[slot]).start()

        def scatter_chunk(c, carry):
            r0 = c * SCATTER_CHUNK
            r1 = r0 + SCATTER_CHUNK
            all_prompt = r1 <= n_prompt
            all_sample = (r0 >= n_prompt) & (r1 <= n_valid)
            all_pad = r0 >= n_valid

            @pl.when(all_prompt)
            def _():
                for k in range(SCATTER_CHUNK):
                    to_prompt(r0 + k)

            @pl.when(all_sample)
            def _():
                for k in range(SCATTER_CHUNK):
                    to_sample(r0 + k)

            @pl.when(all_pad)
            def _():
                for k in range(SCATTER_CHUNK):
                    to_trash(r0 + k)

            @pl.when(jnp.logical_not(all_prompt | all_sample | all_pad))
            def _():
                for k in range(SCATTER_CHUNK):
                    r = r0 + k
                    pl.when(r < n_prompt)(functools.partial(to_prompt, r))
                    pl.when((r >= n_prompt) & (r < n_valid))(functools.partial(to_sample, r))
                    pl.when(r >= n_valid)(functools.partial(to_trash, r))

            return carry

        lax.fori_loop(0, tm // SCATTER_CHUNK, scatter_chunk, 0)

        @pl.when(nvalid_ref[t + 1] == 0)
        def _():
            wait_scatter(slot)

    for slot in (0, 1):
        parity = t % 2 == slot
        pl.when(prev_live & jnp.logical_not(live) & parity)(functools.partial(wait_gather, slot))
        pl.when(live & parity)(functools.partial(tile, slot))


def _moe(h_ext, elo, ehi, nvalid, nprompt, src, n2g, nfg, wg, wu, wd, tm, n_p):
    n, dx = h_ext.shape
    d = wg.shape[1]
    f = wg.shape[-1]
    n_t = src.shape[0]
    lo = lambda t, elo, ehi, nv, npr: (elo[t], 0, 0)
    hi = lambda t, elo, ehi, nv, npr: (ehi[t], 0, 0)
    const = lambda t, *_: (0, 0)
    grid_spec = pltpu.PrefetchScalarGridSpec(
        num_scalar_prefetch=4,
        grid=(n_t,),
        in_specs=[
            pl.BlockSpec((1, 1, tm), lambda t, *_: (t, 0, 0), memory_space=pltpu.SMEM),
            pl.BlockSpec((1, 1, tm), lambda t, *_: (jnp.minimum(t + 1, n_t - 1), 0, 0), memory_space=pltpu.SMEM),
            pl.BlockSpec(memory_space=pl.ANY),
            pl.BlockSpec((1, d), const),
            pl.BlockSpec((1, d), const),
            pl.BlockSpec((1, d, f), lo),
            pl.BlockSpec((1, d, f), lo),
            pl.BlockSpec((1, f, d), lo),
            pl.BlockSpec((1, d, f), hi),
            pl.BlockSpec((1, d, f), hi),
            pl.BlockSpec((1, f, d), hi),
        ],
        out_specs=[pl.BlockSpec(memory_space=pl.ANY), pl.BlockSpec(memory_space=pl.ANY)],
        scratch_shapes=[
            pltpu.VMEM((tm, dx), F32),
            pltpu.VMEM((tm, dx), F32),
            pltpu.VMEM((tm, d), F32),
            pltpu.VMEM((tm, d), F32),
            pltpu.VMEM((tm, d), F32),
            pltpu.SemaphoreType.DMA((2,)),
            pltpu.SemaphoreType.DMA((2,)),
        ],
    )
    return pl.pallas_call(
        functools.partial(_moe_kernel, tm=tm, n_p=n_p),
        grid_spec=grid_spec,
        out_shape=[jax.ShapeDtypeStruct((n_p, d), F32), jax.ShapeDtypeStruct((n - n_p, d), F32)],
        compiler_params=_cparams("arbitrary"),
        name="moe",
    )(elo, ehi, nvalid, nprompt, src, src, h_ext, n2g, nfg, wg, wu, wd, wg, wu, wd)


def _sort_by_class(route, tm, n_p):
    n = route.shape[1]
    n_t = n // tm + N_CLASSES + 1
    cls = route[0].astype(jnp.int32)
    onehot = (cls[:, None] == jnp.arange(N_CLASSES, dtype=jnp.int32)[None, :]).astype(jnp.int32)
    rank = jnp.sum((jnp.cumsum(onehot, axis=0) - onehot) * onehot, axis=1)
    tiles_per = (jnp.sum(onehot, axis=0) + tm - 1) // tm
    tile_end = jnp.cumsum(tiles_per)
    nused = tile_end[-1]
    pos = (tile_end - tiles_per)[cls] * tm + rank
    tile_ids = jnp.minimum(jnp.arange(n_t, dtype=jnp.int32), nused - 1)
    tile_cls = jnp.minimum(jnp.sum((tile_ids[:, None] >= tile_end[None, :]).astype(jnp.int32), axis=1), N_CLASSES - 1)
    pair_lo = jnp.array([0, 0, 0, 1, 1, 2], jnp.int32)
    pair_hi = jnp.array([1, 2, 3, 2, 3, 3], jnp.int32)
    elo = (tile_cls // N_PAIRS) * EXPERTS_PER_GROUP + pair_lo[tile_cls % N_PAIRS]
    ehi = (tile_cls // N_PAIRS) * EXPERTS_PER_GROUP + pair_hi[tile_cls % N_PAIRS]
    p = n_t * tm
    src = jnp.full((p,), -1, jnp.int32).at[pos].set(jnp.arange(n, dtype=jnp.int32))
    nvalid = jnp.sum((src >= 0).reshape(n_t, tm), axis=1).astype(jnp.int32)
    nprompt = jnp.sum(((src >= 0) & (src < n_p)).reshape(n_t, tm), axis=1).astype(jnp.int32)
    src = jnp.maximum(src, 0)
    return elo, ehi, nvalid, nprompt, src.reshape(n_t, 1, tm)


def kernel(x_prompt, x_sample, cache_sb_k, cache_sb_v, state_ret, meta, norm1_g, w_in, w_sb_o, w_ret_o,
           w_out, norm2_g, w_grp, b_grp, w_exp, b_exp, w_gate, w_up, w_down, normf_g):
    batch, seq, d = x_prompt.shape
    dec_b, dec_n, _ = x_sample.shape
    depth, _, past = cache_sb_k.shape[:3]
    assert depth == 1
    n_p, n_s = batch * seq, dec_b * dec_n
    sbw, qkw, vw = HEADS * SB_DH, HEADS * RET_DK, HEADS * RET_DV
    tm = min(ROW_TILE, n_s)

    xp = x_prompt.reshape(n_p, d)
    xs = x_sample.reshape(n_s, d)
    u, sb_q16 = _norm_and_project(xp, xs, meta, norm1_g, w_in[0], sbw, tm)
    meta_row0 = n_p + n_s

    assert seq % tm == 0 and tm % dec_n == 0
    pos = jnp.concatenate([
        jnp.arange(seq, dtype=F32),
        jnp.tile(past + jnp.arange(dec_n, dtype=F32), tm // dec_n),
        jnp.arange(tm, dtype=F32) - N_META,
    ])
    inv_freq = 1.0 / (ROPE_BASE ** (jnp.arange(0, RET_DK, 2, dtype=F32) / RET_DK))
    ang = pos[:, None] * inv_freq[None, :]
    cos_t = jnp.concatenate([jnp.cos(ang), jnp.cos(ang)], axis=1)
    sin_t = jnp.concatenate([-jnp.sin(ang), jnp.sin(ang)], axis=1)
    seq_tiles, n_pt, n_st = seq // tm, n_p // tm, n_s // tm

    def table_block(i):
        return jnp.where(i < n_pt, i % seq_tiles, jnp.where(i < n_pt + n_st, seq_tiles, seq_tiles + 1))

    proj = functools.partial(_project, u, w_in[0], tm=tm, tn=COL_TILE)
    c = sbw
    sb_k16, k_prompt, k_sample = _project_kv(u, w_in[0], c, tm, batch, seq, n_s); c += sbw
    sb_v16, v_prompt, v_sample = _project_kv(u, w_in[0], c, tm, batch, seq, n_s); c += sbw
    rot = (cos_t, sin_t, table_block)
    r_q16 = proj(c, qkw, "rot", rot=rot, out_dtype=BF16); c += qkw
    r_k32 = proj(c, qkw, "rot", rot=rot, scale=RET_DK ** -0.5, out_dtype=F32); c += qkw
    r_v16 = proj(c, vw, "bf16"); c += vw
    gates = proj(c, vw + 2 * d, "f32")

    o_sb_p = _sb_prompt(sb_q16, sb_k16, sb_v16, batch, seq, meta_row0, min(SB_TILE, seq))
    o_sb_s = _sb_sample(sb_q16, sb_k16, sb_v16, cache_sb_k, cache_sb_v, dec_n, n_p, SB_TILE)

    log_g = jnp.log1p(-jnp.power(2.0, -5.0 - jnp.arange(HEADS, dtype=F32)))
    ret = functools.partial(_retention, log_g, r_q16, r_k32, r_v16, gates)
    zero_state = jnp.zeros((1, HEADS, RET_DK, RET_DV), F32)
    _, s_meta = ret(zero_state, 1, N_META, meta_row0, N_META, True)
    o_ret_p, s_p = ret(s_meta, batch, seq, 0, min(RET_CHUNK, seq), True)
    o_ret_s, s_s = ret(state_ret[0], dec_b, dec_n, n_p, dec_n, False)

    n_route = N_GROUPS + N_GROUPS * EXPERTS_PER_GROUP
    w_router = jnp.pad(jnp.concatenate([w_grp[0], w_exp[0]], axis=1), ((0, 0), (0, 128 - n_route)))
    b_router = jnp.pad(jnp.concatenate([b_grp[0], b_exp[0]]), (0, 128 - n_route)).reshape(1, 128)
    tmm = min(MERGE_TILE, n_s)
    merged = _gate_merge(o_sb_p, o_sb_s, o_ret_p, o_ret_s, gates,
                         w_sb_o[0].astype(BF16), w_ret_o[0].astype(BF16), tmm, COL_TILE)
    wr_hi = w_router.astype(BF16)
    wr_lo = (w_router - wr_hi.astype(F32)).astype(BF16)
    wr_cat = jnp.concatenate([wr_hi, wr_lo], axis=1)
    h, route = _out_proj(merged, xp, xs, w_out[0].astype(BF16), norm2_g, wr_cat, b_router, tmm)

    tmo = min(MOE_TILE, n_s)
    elo, ehi, nvalid, nprompt, src = _sort_by_class(route, tmo, n_p)
    y_p, y_s = _moe(h, elo, ehi, nvalid, nprompt, src, norm2_g, normf_g.reshape(1, d),
                    w_gate[0].astype(BF16), w_up[0].astype(BF16), w_down[0].astype(BF16), tmo, n_p)
    y_prompt = y_p.reshape(batch, seq, d)
    y_sample = y_s.reshape(dec_b, dec_n, d)

    pshape = (1, batch, N_META + seq, HEADS, SB_DH)
    sshape = (1, dec_b, dec_n, HEADS, SB_DH)
    return (y_prompt, y_sample, k_prompt.reshape(pshape), v_prompt.reshape(pshape), s_p[None],
            k_sample.reshape(sshape), v_sample.reshape(sshape), s_s[None])
```

```python
import functools

import jax
import jax.numpy as jnp
from jax import lax
from jax.experimental import pallas as pl
from jax.experimental.pallas import tpu as pltpu

F32 = jnp.float32
BF16 = jnp.bfloat16

N_META = 16
HEADS = 8
SB_DH = 128
RET_DK = 128
RET_DV = 256
N_GROUPS = 4
EXPERTS_PER_GROUP = 4
N_PAIRS = 6
N_CLASSES = N_GROUPS * N_PAIRS
ROPE_BASE = 10000.0
EPS = 1e-6
SB_LOG_CUTOFF = -88.0

ROW_TILE = 512
COL_TILE = 1024
MERGE_TILE = 512
MOE_TILE = 256
SCATTER_CHUNK = 8
SB_TILE = 128
RET_CHUNK = 256
VMEM_LIMIT = 56 * 1024 * 1024


def _cparams(*sem):
    return pltpu.CompilerParams(dimension_semantics=sem, vmem_limit_bytes=VMEM_LIMIT)


def _rms(x, g):
    return (x * lax.rsqrt(jnp.mean(x * x, axis=-1, keepdims=True) + EPS)) * g


def _dot(a, b):
    return jnp.dot(a, b, preferred_element_type=F32)


def _dot_nt(a, b):
    return lax.dot_general(a, b, (((1,), (1,)), ((), ())), preferred_element_type=F32)


def _dot_tn(a, b):
    return lax.dot_general(a, b, (((0,), (0,)), ((), ())), preferred_element_type=F32)


def _norm_proj_kernel(xp_ref, xs_ref, meta_ref, g_ref, w_ref, u_ref, o_ref, w16, *, n_p, n_s):
    i = pl.program_id(0)

    @pl.when(i == 0)
    def _():
        w16[...] = w_ref[...].astype(BF16)

    @pl.when(i < n_p)
    def _():
        u_ref[...] = _rms(xp_ref[...], g_ref[...]).astype(BF16)

    @pl.when((i >= n_p) & (i < n_p + n_s))
    def _():
        u_ref[...] = _rms(xs_ref[...], g_ref[...]).astype(BF16)

    @pl.when(i >= n_p + n_s)
    def _():
        u_ref[...] = jnp.zeros_like(u_ref)
        u_ref[0:N_META, :] = _rms(meta_ref[...], g_ref[...]).astype(BF16)

    o_ref[...] = _dot(u_ref[...], w16[...]).astype(BF16)


def _norm_and_project(xp, xs, meta, g, w, width, tm):
    d = xp.shape[1]
    n_p, n_s = xp.shape[0] // tm, xs.shape[0] // tm
    m = (n_p + n_s + 1) * tm
    return pl.pallas_call(
        functools.partial(_norm_proj_kernel, n_p=n_p, n_s=n_s),
        grid=(n_p + n_s + 1,),
        in_specs=[
            pl.BlockSpec((tm, d), lambda i: (jnp.minimum(i, n_p - 1), 0)),
            pl.BlockSpec((tm, d), lambda i: (jnp.clip(i - n_p, 0, n_s - 1), 0)),
            pl.BlockSpec((N_META, d), lambda i: (0, 0)),
            pl.BlockSpec((1, d), lambda i: (0, 0)),
            pl.BlockSpec((d, width), lambda i: (0, 0), pipeline_mode=pl.Buffered(1)),
        ],
        out_specs=[pl.BlockSpec((tm, d), lambda i: (i, 0)), pl.BlockSpec((tm, width), lambda i: (i, 0))],
        out_shape=[jax.ShapeDtypeStruct((m, d), BF16), jax.ShapeDtypeStruct((m, width), BF16)],
        scratch_shapes=[pltpu.VMEM((d, width), BF16)],
        compiler_params=_cparams("arbitrary"),
        name="norm_proj",
    )(xp, xs, meta, g, w)


def _proj_kernel(u_ref, w_ref, *refs, mode, scale, rider_blocks):
    w16 = refs[-1]
    n_rider = len(rider_blocks)
    riders_in, refs = refs[:n_rider], refs[n_rider:]

    @pl.when(pl.program_id(1) == 0)
    def _():
        w16[...] = w_ref[...].astype(BF16)

    riders_out = refs[len(refs) - 1 - n_rider:len(refs) - 1]
    for src, dst in zip(riders_in, riders_out):
        dst[...] = src[...].astype(BF16)

    acc = _dot(u_ref[...], w16[...])
    if mode == "f32":
        refs[0][...] = acc
    elif mode == "bf16":
        refs[0][...] = acc.astype(BF16)
    else:
        cos_ref, sin_ref, o_ref = refs[:3]
        cos, sin = cos_ref[...], sin_ref[...]
        for hh in range(acc.shape[1] // RET_DK):
            x = acc[:, hh * RET_DK:(hh + 1) * RET_DK]
            r = x * cos + pltpu.roll(x, RET_DK // 2, 1) * sin
            if scale is not None:
                r = r * scale
            o_ref[:, hh * RET_DK:(hh + 1) * RET_DK] = r.astype(o_ref.dtype)


def _project(u, w, col0, width, mode, tm, tn, rot=None, scale=None, out_dtype=None, riders=()):
    m, d = u.shape
    tn = min(tn, width)
    n_j, n_i = width // tn, m // tm
    j0 = col0 // tn
    in_specs = [
        pl.BlockSpec((tm, d), lambda j, i: (i, 0)),
        pl.BlockSpec((d, tn), lambda j, i: (0, j0 + j)),
    ]
    args = [u, w]
    rider_blocks, rider_specs = [], []
    for a in riders:
        nb = max(b for b in (1, 2, 4, 8, 16, 32, 64, 128) if b <= n_j * n_i and a.shape[0] % (16 * b) == 0)
        spec = pl.BlockSpec((a.shape[0] // nb, a.shape[1]), lambda j, i, nb=nb: (jnp.minimum(j * n_i + i, nb - 1), 0))
        rider_blocks.append(nb)
        rider_specs.append(spec)
    in_specs += rider_specs
    args += list(riders)
    out_specs = [pl.BlockSpec((tm, tn), lambda j, i: (i, j))]
    if mode == "rot":
        cos_t, sin_t, table_block = rot
        in_specs += [pl.BlockSpec((tm, RET_DK), lambda j, i: (table_block(i), 0))] * 2
        args += [cos_t, sin_t]
        out_shape = [jax.ShapeDtypeStruct((m, width), out_dtype)]
    else:
        out_shape = [jax.ShapeDtypeStruct((m, width), F32 if mode == "f32" else BF16)]
    out_specs += rider_specs
    out_shape += [jax.ShapeDtypeStruct(a.shape, BF16) for a in riders]
    res = pl.pallas_call(
        functools.partial(_proj_kernel, mode=mode, scale=scale, rider_blocks=tuple(rider_blocks)),
        grid=(n_j, n_i),
        in_specs=in_specs,
        out_specs=out_specs,
        out_shape=out_shape,
        scratch_shapes=[pltpu.VMEM((d, tn), BF16)],
        compiler_params=_cparams("arbitrary", "arbitrary"),
        name="proj_" + mode,
    )(*args)
    return res if riders else res[0]


def _proj_kv_kernel(u_ref, w_ref, o16_ref, outp_hbm, outs_hbm, w16, stage, sem,
                    *, tm, n_pt, n_st, tiles_per_batch, batch, rows_per_batch):
    i = pl.program_id(0)

    @pl.when(i == 0)
    def _():
        w16[...] = w_ref[...].astype(BF16)

    acc = _dot(u_ref[...], w16[...])
    o16_ref[...] = acc.astype(BF16)

    def tile_copy(dst):
        return pltpu.make_async_copy(stage, dst, sem)

    @pl.when(i > 0)
    def _():
        tile_copy(outp_hbm.at[pl.ds(0, tm)]).wait()

    for hd in range(HEADS):
        stage[:, hd, :] = acc[:, hd * SB_DH:(hd + 1) * SB_DH]

    @pl.when(i < n_pt)
    def _():
        b = i // tiles_per_batch
        r0 = (i % tiles_per_batch) * tm
        tile_copy(outp_hbm.at[pl.ds(b * rows_per_batch + N_META + r0, tm)]).start()

    @pl.when((i >= n_pt) & (i < n_pt + n_st))
    def _():
        tile_copy(outs_hbm.at[pl.ds((i - n_pt) * tm, tm)]).start()

    @pl.when(i == n_pt + n_st)
    def _():
        copies = [pltpu.make_async_copy(stage.at[pl.ds(0, N_META)],
                                        outp_hbm.at[pl.ds(b * rows_per_batch, N_META)], sem)
                  for b in range(batch)]
        for c in copies:
            c.start()
        for c in copies:
            c.wait()


def _project_kv(u, w, col0, tm, batch, seq, n_s):
    m, d = u.shape
    width = HEADS * SB_DH
    n_pt, n_st = batch * seq // tm, n_s // tm
    assert seq % tm == 0 and m // tm == n_pt + n_st + 1
    j0 = col0 // width
    rows_per_batch = N_META + seq
    return pl.pallas_call(
        functools.partial(_proj_kv_kernel, tm=tm, n_pt=n_pt, n_st=n_st, tiles_per_batch=seq // tm,
                          batch=batch, rows_per_batch=rows_per_batch),
        grid=(m // tm,),
        in_specs=[
            pl.BlockSpec((tm, d), lambda i: (i, 0)),
            pl.BlockSpec((d, width), lambda i: (0, j0), pipeline_mode=pl.Buffered(1)),
        ],
        out_specs=[
            pl.BlockSpec((tm, width), lambda i: (i, 0)),
            pl.BlockSpec(memory_space=pl.ANY),
            pl.BlockSpec(memory_space=pl.ANY),
        ],
        out_shape=[
            jax.ShapeDtypeStruct((m, width), BF16),
            jax.ShapeDtypeStruct((batch * rows_per_batch, HEADS, SB_DH), F32),
            jax.ShapeDtypeStruct((n_s, HEADS, SB_DH), F32),
        ],
        scratch_shapes=[
            pltpu.VMEM((d, width), BF16),
            pltpu.VMEM((tm, HEADS, SB_DH), F32),
            pltpu.SemaphoreType.DMA(()),
        ],
        compiler_params=_cparams("arbitrary"),
        name="proj_kv",
    )(u, w)


def _upper_ones(n):
    r = lax.broadcasted_iota(jnp.int32, (n, n), 0)
    c = lax.broadcasted_iota(jnp.int32, (n, n), 1)
    return (r > c).astype(BF16)


def _sb_tiles(tiles, tri):
    n = range(len(tiles))
    q, k, v, carry, acc, valid = zip(*tiles)
    z = [_dot_nt(q[i], k[i]) * (SB_DH ** -0.5) for i in n]
    e = [jnp.exp(-jnp.abs(z[i])) for i in n]
    sp = [jnp.maximum(z[i], 0.0) + jnp.log(1.0 + e[i]) for i in n]
    lf = [-sp[i] if valid[i] is None else jnp.where(valid[i], -sp[i], 0.0) for i in n]
    hi = [lf[i].astype(BF16) for i in n]
    lo = [(lf[i] - hi[i].astype(F32)).astype(BF16) for i in n]
    later = [_dot(hi[i], tri) + _dot(lo[i], tri) for i in n]
    a = [jnp.exp((z[i] - sp[i]) + (later[i] + carry[i])) for i in n]
    a = [a[i] if valid[i] is None else jnp.where(valid[i], a[i], 0.0) for i in n]
    acc = [acc[i] + _dot(a[i].astype(BF16), v[i]) for i in n]
    carry = [carry[i] + jnp.sum(lf[i], axis=-1, keepdims=True) for i in n]
    return list(zip(carry, acc))


def _sb_tile(q, k, v, carry, acc, tri, valid=None):
    return _sb_tiles([(q, k, v, carry, acc, valid)], tri)[0]


def _sb_prompt_kernel(q_ref, k_ref, v_ref, mk_ref, mv_ref, o_ref, *, t, n_q, unroll):
    tri = _upper_ones(t)
    tri2 = _upper_ones(2 * t)
    tri_m = _upper_ones(N_META)
    col_minus_row = (lax.broadcasted_iota(jnp.int32, (t, 2 * t), 1)
                     - lax.broadcasted_iota(jnp.int32, (t, 2 * t), 0))

    def window(i):
        r0 = pl.multiple_of(i * t, t)
        w0 = pl.multiple_of(jnp.maximum(i - 1, 0) * t, t)
        valid = col_minus_row < (r0 - w0)
        return (q_ref[pl.ds(r0, t), :], k_ref[pl.ds(w0, 2 * t), :], v_ref[pl.ds(w0, 2 * t), :],
                jnp.zeros((t, 1), F32), jnp.zeros((t, SB_DH), F32), valid)

    def older_keys(i, q, carry, acc):
        def cond(s):
            return (s[0] >= 0) & (jnp.max(s[1]) > SB_LOG_CUTOFF)

        def body(s):
            j, carry, acc = s
            c0 = pl.multiple_of(j * t, t)
            carry, acc = _sb_tile(q, k_ref[pl.ds(c0, t), :], v_ref[pl.ds(c0, t), :], carry, acc, tri)
            return j - 1, carry, acc

        _, carry, acc = lax.while_loop(cond, body, (i - 2, carry, acc))
        return lax.cond(
            jnp.max(carry) > SB_LOG_CUTOFF,
            lambda: _sb_tile(q, mk_ref[...], mv_ref[...], carry, acc, tri_m)[1],
            lambda: acc,
        )

    def group(p, _):
        tiles = [window(p * unroll + u) for u in range(unroll)]
        done = _sb_tiles(tiles, tri2)
        for u, (carry, acc) in enumerate(done):
            i = p * unroll + u
            q = tiles[u][0]
            acc = lax.cond(
                jnp.max(carry) > SB_LOG_CUTOFF,
                functools.partial(older_keys, i, q, carry, acc),
                lambda acc=acc: acc,
            )
            o_ref[pl.ds(pl.multiple_of(i * t, t), t), :] = acc.astype(o_ref.dtype)
        return 0

    lax.fori_loop(0, n_q // unroll, group, 0)


def _sb_prompt(q16, k16, v16, batch, seq, meta_row0, t):
    n_q = seq // t
    assert n_q >= 2
    unroll = max(u for u in (1, 2, 4, 8) if n_q % u == 0)
    mb = meta_row0 // N_META
    blk = lambda b, h: (b, h)
    mblk = lambda b, h: (mb, h)
    return pl.pallas_call(
        functools.partial(_sb_prompt_kernel, t=t, n_q=n_q, unroll=unroll),
        grid=(batch, HEADS),
        in_specs=[
            pl.BlockSpec((seq, SB_DH), blk),
            pl.BlockSpec((seq, SB_DH), blk),
            pl.BlockSpec((seq, SB_DH), blk),
            pl.BlockSpec((N_META, SB_DH), mblk),
            pl.BlockSpec((N_META, SB_DH), mblk),
        ],
        out_specs=pl.BlockSpec((seq, SB_DH), blk),
        out_shape=jax.ShapeDtypeStruct((batch * seq, HEADS * SB_DH), BF16),
        compiler_params=_cparams("arbitrary", "arbitrary"),
        name="sb_prompt",
    )(q16, k16, v16, k16, v16)


def _sb_sample_kernel(q_ref, k_ref, v_ref, ckl_ref, cvl_ref, ck_hbm, cv_hbm, o_ref, kbuf, vbuf, sem,
                      *, n, t, n_past):
    b = pl.program_id(0)
    tri_n = _upper_ones(n)
    tri = _upper_ones(t)
    causal = lax.broadcasted_iota(jnp.int32, (n, n), 1) < lax.broadcasted_iota(jnp.int32, (n, n), 0)

    def older_keys(hd, q, carry, acc):
        def cond(s):
            return (s[0] >= 0) & (jnp.max(s[1]) > SB_LOG_CUTOFF)

        def body(s):
            j, carry, acc = s
            rows = pl.ds(pl.multiple_of(j * t, t), t)
            ck = pltpu.make_async_copy(ck_hbm.at[0, b, rows], kbuf, sem.at[0])
            cv = pltpu.make_async_copy(cv_hbm.at[0, b, rows], vbuf, sem.at[1])
            ck.start()
            cv.start()
            ck.wait()
            cv.wait()
            carry, acc = _sb_tile(q, kbuf[:, hd, :].astype(BF16), vbuf[:, hd, :].astype(BF16), carry, acc, tri)
            return j - 1, carry, acc

        return lax.while_loop(cond, body, (n_past - 2, carry, acc))[2]

    cols = [slice(hd * SB_DH, (hd + 1) * SB_DH) for hd in range(HEADS)]
    qs = [q_ref[:, c] for c in cols]
    new = _sb_tiles([(qs[hd], k_ref[:, cols[hd]], v_ref[:, cols[hd]], jnp.zeros((n, 1), F32),
                      jnp.zeros((n, SB_DH), F32), causal) for hd in range(HEADS)], tri_n)
    old = _sb_tiles([(qs[hd], ckl_ref[0, 0, :, hd, :].astype(BF16), cvl_ref[0, 0, :, hd, :].astype(BF16),
                      new[hd][0], new[hd][1], None) for hd in range(HEADS)], tri)
    for hd, (carry, acc) in enumerate(old):
        q = qs[hd]
        acc = lax.cond(
            jnp.max(carry) > SB_LOG_CUTOFF,
            functools.partial(older_keys, hd, q, carry, acc),
            lambda acc=acc: acc,
        )
        o_ref[:, hd * SB_DH:(hd + 1) * SB_DH] = acc.astype(o_ref.dtype)


def _sb_sample(q16, k16, v16, cache_k, cache_v, n, row0, t):
    _, dec_b, past = cache_k.shape[:3]
    t = min(t, past)
    n_past = past // t
    rb = row0 // n
    w = HEADS * SB_DH
    blk = lambda b: (rb + b, 0)
    last = lambda b: (0, b, n_past - 1, 0, 0)
    return pl.pallas_call(
        functools.partial(_sb_sample_kernel, n=n, t=t, n_past=n_past),
        grid=(dec_b,),
        in_specs=[
            pl.BlockSpec((n, w), blk),
            pl.BlockSpec((n, w), blk),
            pl.BlockSpec((n, w), blk),
            pl.BlockSpec((1, 1, t, HEADS, SB_DH), last),
            pl.BlockSpec((1, 1, t, HEADS, SB_DH), last),
            pl.BlockSpec(memory_space=pl.ANY),
            pl.BlockSpec(memory_space=pl.ANY),
        ],
        out_specs=pl.BlockSpec((n, w), lambda b: (b, 0)),
        out_shape=jax.ShapeDtypeStruct((dec_b * n, w), BF16),
        scratch_shapes=[
            pltpu.VMEM((t, HEADS, SB_DH), F32),
            pltpu.VMEM((t, HEADS, SB_DH), F32),
            pltpu.SemaphoreType.DMA((2,)),
        ],
        compiler_params=_cparams("arbitrary"),
        name="sb_sample",
    )(q16, k16, v16, cache_k, cache_v, cache_k, cache_v)


def _ret_kernel(lg_ref, q_ref, k_ref, v_ref, g_ref, s0_ref, o_ref, s_out_ref, s_scr, d_scr, *, c, n_c):
    ci = pl.program_id(1)

    @pl.when((pl.program_id(0) == 0) & (ci == 0))
    def _():
        rel = (lax.broadcasted_iota(jnp.int32, (c, c), 0) - lax.broadcasted_iota(jnp.int32, (c, c), 1)).astype(F32)
        for hd in range(HEADS):
            d_scr[hd] = jnp.where(rel >= 0, jnp.exp(jnp.maximum(rel, 0.0) * lg_ref[hd]), 0.0)

    @pl.when(ci == 0)
    def _():
        s_scr[...] = s0_ref[0]

    tcol = lax.broadcasted_iota(jnp.int32, (c, 1), 0).astype(F32)
    for hd in range(HEADS):
        lg = lg_ref[hd]
        qk = slice(hd * RET_DK, (hd + 1) * RET_DK)
        vv = slice(hd * RET_DV, (hd + 1) * RET_DV)
        q = q_ref[:, qk]
        kf = k_ref[:, qk]
        v = v_ref[:, vv]
        s = s_scr[hd]
        scores = _dot_nt(q, kf.astype(BF16)) * d_scr[hd]
        o = _dot(scores.astype(BF16), v) + _dot(q, s.astype(BF16)) * jnp.exp((tcol + 1.0) * lg)
        k_dec = (kf * jnp.exp((c - 1.0 - tcol) * lg)).astype(BF16)
        s_scr[hd] = jnp.exp(c * lg) * s + _dot_tn(k_dec, v)
        o = o * lax.rsqrt(jnp.mean(o * o, axis=-1, keepdims=True) + EPS)
        g = g_ref[:, vv]
        o_ref[:, vv] = (o * (g * jax.nn.sigmoid(g))).astype(o_ref.dtype)

    @pl.when(ci == n_c - 1)
    def _():
        s_out_ref[0] = s_scr[...]


def _retention(log_g, rq16, rk32, rv16, gates, s0, batch, rows_per_batch, row0, c, shared_s0):
    n_c = rows_per_batch // c
    rb = row0 // c
    qkw, vw = HEADS * RET_DK, HEADS * RET_DV
    row = lambda b, ci: (rb + b * n_c + ci, 0)
    sidx = (lambda b, ci: (0, 0, 0, 0)) if shared_s0 else (lambda b, ci: (b, 0, 0, 0))
    return pl.pallas_call(
        functools.partial(_ret_kernel, c=c, n_c=n_c),
        grid=(batch, n_c),
        in_specs=[
            pl.BlockSpec(memory_space=pltpu.SMEM),
            pl.BlockSpec((c, qkw), row),
            pl.BlockSpec((c, qkw), row),
            pl.BlockSpec((c, vw), row),
            pl.BlockSpec((c, vw), row),
            pl.BlockSpec((1, HEADS, RET_DK, RET_DV), sidx),
        ],
        out_specs=[
            pl.BlockSpec((c, vw), lambda b, ci: (b * n_c + ci, 0)),
            pl.BlockSpec((1, HEADS, RET_DK, RET_DV), lambda b, ci: (b, 0, 0, 0)),
        ],
        out_shape=[
            jax.ShapeDtypeStruct((batch * rows_per_batch, vw), BF16),
            jax.ShapeDtypeStruct((batch, HEADS, RET_DK, RET_DV), F32),
        ],
        scratch_shapes=[pltpu.VMEM((HEADS, RET_DK, RET_DV), F32), pltpu.VMEM((HEADS, c, c), F32)],
        compiler_params=_cparams("arbitrary", "arbitrary"),
        name="retention",
    )(log_g, rq16, rk32, rv16, gates, s0)


def _route(logits):
    gl = [logits[r:r + 1, :] for r in range(N_GROUPS)]
    gmax = functools.reduce(jnp.maximum, gl)
    g_idx = jnp.where(gl[0] == gmax, 0, jnp.where(gl[1] == gmax, 1, jnp.where(gl[2] == gmax, 2, 3)))
    g_top = 1.0 / functools.reduce(lambda a, b: a + b, [jnp.exp(x - gmax) for x in gl])
    e = []
    for r in range(EXPERTS_PER_GROUP):
        rows = [logits[N_GROUPS + g * EXPERTS_PER_GROUP + r:N_GROUPS + g * EXPERTS_PER_GROUP + r + 1, :]
                for g in range(N_GROUPS)]
        e.append(jnp.where(g_idx == 0, rows[0], jnp.where(g_idx == 1, rows[1], jnp.where(g_idx == 2, rows[2], rows[3]))))
    m1 = functools.reduce(jnp.maximum, e)
    i1 = jnp.where(e[0] == m1, 0, jnp.where(e[1] == m1, 1, jnp.where(e[2] == m1, 2, 3)))
    e2 = [jnp.where(i1 == r, -jnp.inf, e[r]) for r in range(EXPERTS_PER_GROUP)]
    m2 = functools.reduce(jnp.maximum, e2)
    i2 = jnp.where(e2[0] == m2, 0, jnp.where(e2[1] == m2, 1, jnp.where(e2[2] == m2, 2, 3)))
    p2 = jnp.exp(m2 - m1)
    w1 = g_top / (1.0 + p2)
    w2 = g_top * p2 / (1.0 + p2)
    lo = jnp.minimum(i1, i2)
    hi = jnp.maximum(i1, i2)
    w_lo = jnp.where(i1 < i2, w1, w2)
    w_hi = jnp.where(i1 < i2, w2, w1)
    pair = jnp.where(lo == 0, hi - 1, jnp.where(lo == 1, hi + 1, 5))
    return g_idx * N_PAIRS + pair, w_lo, w_hi


def _gate_merge_kernel(osp_ref, oss_ref, orp_ref, ors_ref, gsb_ref, gret_ref, wsb_ref, wret_ref, m_ref, *, n_p):
    is_p = pl.program_id(1) < n_p
    o_sb = jnp.where(is_p, osp_ref[...], oss_ref[...])
    o_ret = jnp.where(is_p, orp_ref[...], ors_ref[...])
    merged = (jax.nn.sigmoid(gsb_ref[...]) * _dot(o_sb, wsb_ref[...])
              + jax.nn.sigmoid(gret_ref[...]) * _dot(o_ret, wret_ref[...]))
    m_ref[...] = merged.astype(m_ref.dtype)


def _gate_merge(osp, oss, orp, ors, gates, wsb, wret, tm, tn):
    d = wsb.shape[1]
    tn = min(tn, d)
    n_p, n_s = osp.shape[0] // tm, oss.shape[0] // tm
    n_j = d // tn
    vw = HEADS * RET_DV
    pidx = lambda j, i: (jnp.minimum(i, n_p - 1), 0)
    sidx = lambda j, i: (jnp.clip(i - n_p, 0, n_s - 1), 0)
    return pl.pallas_call(
        functools.partial(_gate_merge_kernel, n_p=n_p),
        grid=(n_j, n_p + n_s),
        in_specs=[
            pl.BlockSpec((tm, HEADS * SB_DH), pidx),
            pl.BlockSpec((tm, HEADS * SB_DH), sidx),
            pl.BlockSpec((tm, vw), pidx),
            pl.BlockSpec((tm, vw), sidx),
            pl.BlockSpec((tm, tn), lambda j, i: (i, vw // tn + j)),
            pl.BlockSpec((tm, tn), lambda j, i: (i, (vw + d) // tn + j)),
            pl.BlockSpec((wsb.shape[0], tn), lambda j, i: (0, j)),
            pl.BlockSpec((wret.shape[0], tn), lambda j, i: (0, j)),
        ],
        out_specs=pl.BlockSpec((tm, tn), lambda j, i: (i, j)),
        out_shape=jax.ShapeDtypeStruct(((n_p + n_s) * tm, d), BF16),
        compiler_params=_cparams("arbitrary", "arbitrary"),
        name="gate_merge",
    )(osp, oss, orp, ors, gates, gates, wsb, wret)


def _out_kernel(m_ref, xp_ref, xs_ref, wout_ref, n2g_ref, wrc_ref, br_ref, h_ref, route_ref, *, n_p):
    x = jnp.where(pl.program_id(0) < n_p, xp_ref[...], xs_ref[...])
    d = x.shape[1]
    h = x + _dot(m_ref[...], wout_ref[...])
    h_ref[:, :d] = h
    u2 = _rms(h, n2g_ref[...])
    u_hi = u2.astype(BF16)
    u_lo = (u2 - u_hi.astype(F32)).astype(BF16)
    hi_both = _dot(u_hi, wrc_ref[...])
    logits = (hi_both[:, :128] + (hi_both[:, 128:] + _dot(u_lo, wrc_ref[:, :128]))) + br_ref[...]
    cls, w_lo, w_hi = _route(logits.T)
    route_ref[...] = jnp.zeros_like(route_ref)
    route_ref[0:1, :] = cls.astype(F32)
    w_rows = jnp.concatenate([w_lo, w_hi, jnp.zeros((126, w_lo.shape[1]), F32)], axis=0)
    h_ref[:, d:] = w_rows.T


def _out_proj(m, xp, xs, wout, n2g, wr_cat, br, tm):
    d = xp.shape[1]
    n_p, n_s = xp.shape[0] // tm, xs.shape[0] // tm
    n = n_p + n_s
    const = lambda i: (0, 0)
    return pl.pallas_call(
        functools.partial(_out_kernel, n_p=n_p),
        grid=(n,),
        in_specs=[
            pl.BlockSpec((tm, d), lambda i: (i, 0)),
            pl.BlockSpec((tm, d), lambda i: (jnp.minimum(i, n_p - 1), 0)),
            pl.BlockSpec((tm, d), lambda i: (jnp.clip(i - n_p, 0, n_s - 1), 0)),
            pl.BlockSpec(wout.shape, const, pipeline_mode=pl.Buffered(1)),
            pl.BlockSpec((1, d), const),
            pl.BlockSpec(wr_cat.shape, const),
            pl.BlockSpec(br.shape, const),
        ],
        out_specs=[
            pl.BlockSpec((tm, d + 128), lambda i: (i, 0)),
            pl.BlockSpec((8, tm), lambda i: (0, i)),
        ],
        out_shape=[
            jax.ShapeDtypeStruct((n * tm, d + 128), F32),
            jax.ShapeDtypeStruct((8, n * tm), F32),
        ],
        compiler_params=_cparams("arbitrary"),
        name="out_proj",
    )(m, xp, xs, wout, n2g, wr_cat, br)


def _moe_kernel(elo_ref, ehi_ref, nvalid_ref, nprompt_ref, src_ref, nxt_ref, h_hbm, n2g_ref, nfg_ref,
                wg_lo, wu_lo, wd_lo, wg_hi, wu_hi, wd_hi, yp_hbm, ys_hbm, hbuf0, hbuf1, ybuf0, ybuf1, trash,
                sem_in, sem_out, *, tm, n_p):
    t = pl.program_id(0)
    n_valid = nvalid_ref[t]
    n_prompt = nprompt_ref[t]
    live = n_valid > 0
    prev_live = (t > 0) & (nvalid_ref[jnp.maximum(t - 1, 0)] > 0)
    hbuf, ybuf = (hbuf0, hbuf1), (ybuf0, ybuf1)
    d = ybuf0.shape[-1]

    def start_gather(idx_ref, s):
        for r in range(tm):
            pltpu.make_async_copy(h_hbm.at[pl.ds(idx_ref[0, 0, r], 1), :], hbuf[s].at[pl.ds(r, 1), :],
                                  sem_in.at[s]).start()

    def wait_gather(s):
        pltpu.make_async_copy(h_hbm.at[pl.ds(0, tm), :], hbuf[s], sem_in.at[s]).wait()

    def wait_scatter(s):
        pltpu.make_async_copy(ybuf[s], yp_hbm.at[pl.ds(0, tm), :], sem_out.at[s]).wait()

    @pl.when(t == 0)
    def _():
        start_gather(src_ref, 0)

    def tile(slot):
        wait_gather(slot)
        start_gather(nxt_ref, 1 - slot)
        hrows = hbuf[slot][:, :d]
        wts = hbuf[slot][:, d:]
        u = _rms(hrows, n2g_ref[...]).astype(BF16)
        y = jnp.zeros_like(hrows)
        for col, (wg, wu, wd) in enumerate(((wg_lo, wu_lo, wd_lo), (wg_hi, wu_hi, wd_hi))):
            gate = _dot(u, wg[0])
            hid = (gate * jax.nn.sigmoid(gate)) * _dot(u, wu[0])
            y = y + wts[:, col:col + 1] * _dot(hid.astype(BF16), wd[0])
        ybuf[slot][...] = _rms(hrows + y, nfg_ref[...])

        @pl.when(prev_live)
        def _():
            wait_scatter(1 - slot)

        def to_prompt(r):
            pltpu.make_async_copy(ybuf[slot].at[pl.ds(r, 1), :], yp_hbm.at[pl.ds(src_ref[0, 0, r], 1), :],
                                  sem_out.at[slot]).start()

        def to_sample(r):
            pltpu.make_async_copy(ybuf[slot].at[pl.ds(r, 1), :], ys_hbm.at[pl.ds(src_ref[0, 0, r] - n_p, 1), :],
                                  sem_out.at[slot]).start()

        def to_trash(r):
            pltpu.make_async_copy(ybuf[slot].at[pl.ds(r, 1), :], trash.at[pl.ds(r, 1), :],
                                  sem_out.at[slot]).start()

        def scatter_chunk(c, carry):
            r0 = c * SCATTER_CHUNK
            r1 = r0 + SCATTER_CHUNK
            all_prompt = r1 <= n_prompt
            all_sample = (r0 >= n_prompt) & (r1 <= n_valid)
            all_pad = r0 >= n_valid

            @pl.when(all_prompt)
            def _():
                for k in range(SCATTER_CHUNK):
                    to_prompt(r0 + k)

            @pl.when(all_sample)
            def _():
                for k in range(SCATTER_CHUNK):
                    to_sample(r0 + k)

            @pl.when(all_pad)
            def _():
                for k in range(SCATTER_CHUNK):
                    to_trash(r0 + k)

            @pl.when(jnp.logical_not(all_prompt | all_sample | all_pad))
            def _():
                for k in range(SCATTER_CHUNK):
                    r = r0 + k
                    pl.when(r < n_prompt)(functools.partial(to_prompt, r))
                    pl.when((r >= n_prompt) & (r < n_valid))(functools.partial(to_sample, r))
                    pl.when(r >= n_valid)(functools.partial(to_trash, r))

            return carry

        lax.fori_loop(0, tm // SCATTER_CHUNK, scatter_chunk, 0)

        @pl.when(nvalid_ref[t + 1] == 0)
        def _():
            wait_scatter(slot)

    for slot in (0, 1):
        parity = t % 2 == slot
        pl.when(prev_live & jnp.logical_not(live) & parity)(functools.partial(wait_gather, slot))
        pl.when(live & parity)(functools.partial(tile, slot))


def _moe(h_ext, elo, ehi, nvalid, nprompt, src, n2g, nfg, wg, wu, wd, tm, n_p):
    n, dx = h_ext.shape
    d = wg.shape[1]
    f = wg.shape[-1]
    n_t = src.shape[0]
    lo = lambda t, elo, ehi, nv, npr: (elo[t], 0, 0)
    hi = lambda t, elo, ehi, nv, npr: (ehi[t], 0, 0)
    const = lambda t, *_: (0, 0)
    grid_spec = pltpu.PrefetchScalarGridSpec(
        num_scalar_prefetch=4,
        grid=(n_t,),
        in_specs=[
            pl.BlockSpec((1, 1, tm), lambda t, *_: (t, 0, 0), memory_space=pltpu.SMEM),
            pl.BlockSpec((1, 1, tm), lambda t, *_: (jnp.minimum(t + 1, n_t - 1), 0, 0), memory_space=pltpu.SMEM),
            pl.BlockSpec(memory_space=pl.ANY),
            pl.BlockSpec((1, d), const),
            pl.BlockSpec((1, d), const),
            pl.BlockSpec((1, d, f), lo),
            pl.BlockSpec((1, d, f), lo),
            pl.BlockSpec((1, f, d), lo),
            pl.BlockSpec((1, d, f), hi),
            pl.BlockSpec((1, d, f), hi),
            pl.BlockSpec((1, f, d), hi),
        ],
        out_specs=[pl.BlockSpec(memory_space=pl.ANY), pl.BlockSpec(memory_space=pl.ANY)],
        scratch_shapes=[
            pltpu.VMEM((tm, dx), F32),
            pltpu.VMEM((tm, dx), F32),
            pltpu.VMEM((tm, d), F32),
            pltpu.VMEM((tm, d), F32),
            pltpu.VMEM((tm, d), F32),
            pltpu.SemaphoreType.DMA((2,)),
            pltpu.SemaphoreType.DMA((2,)),
        ],
    )
    return pl.pallas_call(
        functools.partial(_moe_kernel, tm=tm, n_p=n_p),
        grid_spec=grid_spec,
        out_shape=[jax.ShapeDtypeStruct((n_p, d), F32), jax.ShapeDtypeStruct((n - n_p, d), F32)],
        compiler_params=_cparams("arbitrary"),
        name="moe",
    )(elo, ehi, nvalid, nprompt, src, src, h_ext, n2g, nfg, wg, wu, wd, wg, wu, wd)


def _sort_by_class(route, tm, n_p):
    n = route.shape[1]
    n_t = n // tm + N_CLASSES + 1
    cls = route[0].astype(jnp.int32)
    onehot = (cls[:, None] == jnp.arange(N_CLASSES, dtype=jnp.int32)[None, :]).astype(jnp.int32)
    rank = jnp.sum((jnp.cumsum(onehot, axis=0) - onehot) * onehot, axis=1)
    tiles_per = (jnp.sum(onehot, axis=0) + tm - 1) // tm
    tile_end = jnp.cumsum(tiles_per)
    nused = tile_end[-1]
    pos = (tile_end - tiles_per)[cls] * tm + rank
    tile_ids = jnp.minimum(jnp.arange(n_t, dtype=jnp.int32), nused - 1)
    tile_cls = jnp.minimum(jnp.sum((tile_ids[:, None] >= tile_end[None, :]).astype(jnp.int32), axis=1), N_CLASSES - 1)
    pair_lo = jnp.array([0, 0, 0, 1, 1, 2], jnp.int32)
    pair_hi = jnp.array([1, 2, 3, 2, 3, 3], jnp.int32)
    elo = (tile_cls // N_PAIRS) * EXPERTS_PER_GROUP + pair_lo[tile_cls % N_PAIRS]
    ehi = (tile_cls // N_PAIRS) * EXPERTS_PER_GROUP + pair_hi[tile_cls % N_PAIRS]
    p = n_t * tm
    src = jnp.full((p,), -1, jnp.int32).at[pos].set(jnp.arange(n, dtype=jnp.int32))
    nvalid = jnp.sum((src >= 0).reshape(n_t, tm), axis=1).astype(jnp.int32)
    nprompt = jnp.sum(((src >= 0) & (src < n_p)).reshape(n_t, tm), axis=1).astype(jnp.int32)
    src = jnp.maximum(src, 0)
    return elo, ehi, nvalid, nprompt, src.reshape(n_t, 1, tm)


def kernel(x_prompt, x_sample, cache_sb_k, cache_sb_v, state_ret, meta, norm1_g, w_in, w_sb_o, w_ret_o,
           w_out, norm2_g, w_grp, b_grp, w_exp, b_exp, w_gate, w_up, w_down, normf_g):
    batch, seq, d = x_prompt.shape
    dec_b, dec_n, _ = x_sample.shape
    depth, _, past = cache_sb_k.shape[:3]
    assert depth == 1
    n_p, n_s = batch * seq, dec_b * dec_n
    sbw, qkw, vw = HEADS * SB_DH, HEADS * RET_DK, HEADS * RET_DV
    tm = min(ROW_TILE, n_s)

    xp = x_prompt.reshape(n_p, d)
    xs = x_sample.reshape(n_s, d)
    u, sb_q16 = _norm_and_project(xp, xs, meta, norm1_g, w_in[0], sbw, tm)
    meta_row0 = n_p + n_s

    assert seq % tm == 0 and tm % dec_n == 0
    pos = jnp.concatenate([
        jnp.arange(seq, dtype=F32),
        jnp.tile(past + jnp.arange(dec_n, dtype=F32), tm // dec_n),
        jnp.arange(tm, dtype=F32) - N_META,
    ])
    inv_freq = 1.0 / (ROPE_BASE ** (jnp.arange(0, RET_DK, 2, dtype=F32) / RET_DK))
    ang = pos[:, None] * inv_freq[None, :]
    cos_t = jnp.concatenate([jnp.cos(ang), jnp.cos(ang)], axis=1)
    sin_t = jnp.concatenate([-jnp.sin(ang), jnp.sin(ang)], axis=1)
    seq_tiles, n_pt, n_st = seq // tm, n_p // tm, n_s // tm

    def table_block(i):
        return jnp.where(i < n_pt, i % seq_tiles, jnp.where(i < n_pt + n_st, seq_tiles, seq_tiles + 1))

    proj = functools.partial(_project, u, w_in[0], tm=tm, tn=COL_TILE)
    c = sbw
    sb_k16, k_prompt, k_sample = _project_kv(u, w_in[0], c, tm, batch, seq, n_s); c += sbw
    sb_v16, v_prompt, v_sample = _project_kv(u, w_in[0], c, tm, batch, seq, n_s); c += sbw
    rot = (cos_t, sin_t, table_block)
    r_q16 = proj(c, qkw, "rot", rot=rot, out_dtype=BF16); c += qkw
    r_k32 = proj(c, qkw, "rot", rot=rot, scale=RET_DK ** -0.5, out_dtype=F32); c += qkw
    r_v16 = proj(c, vw, "bf16"); c += vw
    n_exp, _, d_exp = w_gate.shape[1:]
    gates, wg16, wu16, wd16, wsb16, wret16, wout16 = proj(
        c, vw + 2 * d, "f32",
        riders=(w_gate.reshape(n_exp * d, d_exp), w_up.reshape(n_exp * d, d_exp), w_down.reshape(n_exp * d_exp, d),
                w_sb_o[0], w_ret_o[0], w_out[0]))
    wg16 = wg16.reshape(n_exp, d, d_exp)
    wu16 = wu16.reshape(n_exp, d, d_exp)
    wd16 = wd16.reshape(n_exp, d_exp, d)

    o_sb_p = _sb_prompt(sb_q16, sb_k16, sb_v16, batch, seq, meta_row0, min(SB_TILE, seq))
    o_sb_s = _sb_sample(sb_q16, sb_k16, sb_v16, cache_sb_k, cache_sb_v, dec_n, n_p, SB_TILE)

    log_g = jnp.log1p(-jnp.power(2.0, -5.0 - jnp.arange(HEADS, dtype=F32)))
    ret = functools.partial(_retention, log_g, r_q16, r_k32, r_v16, gates)
    zero_state = jnp.zeros((1, HEADS, RET_DK, RET_DV), F32)
    _, s_meta = ret(zero_state, 1, N_META, meta_row0, N_META, True)
    o_ret_p, s_p = ret(s_meta, batch, seq, 0, min(RET_CHUNK, seq), True)
    o_ret_s, s_s = ret(state_ret[0], dec_b, dec_n, n_p, dec_n, False)

    n_route = N_GROUPS + N_GROUPS * EXPERTS_PER_GROUP
    w_router = jnp.pad(jnp.concatenate([w_grp[0], w_exp[0]], axis=1), ((0, 0), (0, 128 - n_route)))
    b_router = jnp.pad(jnp.concatenate([b_grp[0], b_exp[0]]), (0, 128 - n_route)).reshape(1, 128)
    tmm = min(MERGE_TILE, n_s)
    merged = _gate_merge(o_sb_p, o_sb_s, o_ret_p, o_ret_s, gates,
                         wsb16, wret16, tmm, COL_TILE)
    wr_hi = w_router.astype(BF16)
    wr_lo = (w_router - wr_hi.astype(F32)).astype(BF16)
    wr_cat = jnp.concatenate([wr_hi, wr_lo], axis=1)
    h, route = _out_proj(merged, xp, xs, wout16, norm2_g, wr_cat, b_router, tmm)

    tmo = min(MOE_TILE, n_s)
    elo, ehi, nvalid, nprompt, src = _sort_by_class(route, tmo, n_p)
    y_p, y_s = _moe(h, elo, ehi, nvalid, nprompt, src, norm2_g, normf_g.reshape(1, d),
                    wg16, wu16, wd16, tmo, n_p)
    y_prompt = y_p.reshape(batch, seq, d)
    y_sample = y_s.reshape(dec_b, dec_n, d)

    pshape = (1, batch, N_META + seq, HEADS, SB_DH)
    sshape = (1, dec_b, dec_n, HEADS, SB_DH)
    return (y_prompt, y_sample, k_prompt.reshape(pshape), v_prompt.reshape(pshape), s_p[None],
            k_sample.reshape(sshape), v_sample.reshape(sshape), s_s[None])
```

```python
import functools

import jax
import jax.numpy as jnp
from jax import lax
from jax.experimental import pallas as pl
from jax.experimental.pallas import tpu as pltpu

F32 = jnp.float32
BF16 = jnp.bfloat16

N_META = 16
HEADS = 8
SB_DH = 128
RET_DK = 128
RET_DV = 256
N_GROUPS = 4
EXPERTS_PER_GROUP = 4
N_PAIRS = 6
N_CLASSES = N_GROUPS * N_PAIRS
ROPE_BASE = 10000.0
EPS = 1e-6
SB_LOG_CUTOFF = -88.0

ROW_TILE = 1024
NORM_TILE = 512
COL_TILE = 1024
MERGE_TILE = 512
MOE_TILE = 256
SCATTER_CHUNK = 16
SB_TILE = 128
RET_CHUNK = 256
VMEM_LIMIT = 56 * 1024 * 1024


def _cparams(*sem):
    return pltpu.CompilerParams(dimension_semantics=sem, vmem_limit_bytes=VMEM_LIMIT)


def _rms(x, g):
    return (x * lax.rsqrt(jnp.mean(x * x, axis=-1, keepdims=True) + EPS)) * g


def _dot(a, b):
    return jnp.dot(a, b, preferred_element_type=F32)


def _dot_nt(a, b):
    return lax.dot_general(a, b, (((1,), (1,)), ((), ())), preferred_element_type=F32)


def _dot_tn(a, b):
    return lax.dot_general(a, b, (((0,), (0,)), ((), ())), preferred_element_type=F32)


def _norm_proj_kernel(xp_ref, xs_ref, meta_ref, g_ref, w_ref, u_ref, o_ref, w16, *, n_p, n_s):
    i = pl.program_id(0)

    @pl.when(i == 0)
    def _():
        w16[...] = w_ref[...].astype(BF16)

    @pl.when(i < n_p)
    def _():
        u_ref[...] = _rms(xp_ref[...], g_ref[...]).astype(BF16)

    @pl.when((i >= n_p) & (i < n_p + n_s))
    def _():
        u_ref[...] = _rms(xs_ref[...], g_ref[...]).astype(BF16)

    @pl.when(i >= n_p + n_s)
    def _():
        u_ref[...] = jnp.zeros_like(u_ref)

    @pl.when(i == n_p + n_s)
    def _():
        u_ref[0:N_META, :] = _rms(meta_ref[...], g_ref[...]).astype(BF16)

    o_ref[...] = _dot(u_ref[...], w16[...]).astype(BF16)


def _norm_and_project(xp, xs, meta, g, w, width, tm, tail_tiles):
    d = xp.shape[1]
    n_p, n_s = xp.shape[0] // tm, xs.shape[0] // tm
    m = (n_p + n_s + tail_tiles) * tm
    return pl.pallas_call(
        functools.partial(_norm_proj_kernel, n_p=n_p, n_s=n_s),
        grid=(n_p + n_s + tail_tiles,),
        in_specs=[
            pl.BlockSpec((tm, d), lambda i: (jnp.minimum(i, n_p - 1), 0)),
            pl.BlockSpec((tm, d), lambda i: (jnp.clip(i - n_p, 0, n_s - 1), 0)),
            pl.BlockSpec((N_META, d), lambda i: (0, 0)),
            pl.BlockSpec((1, d), lambda i: (0, 0)),
            pl.BlockSpec((d, width), lambda i: (0, 0), pipeline_mode=pl.Buffered(1)),
        ],
        out_specs=[pl.BlockSpec((tm, d), lambda i: (i, 0)), pl.BlockSpec((tm, width), lambda i: (i, 0))],
        out_shape=[jax.ShapeDtypeStruct((m, d), BF16), jax.ShapeDtypeStruct((m, width), BF16)],
        scratch_shapes=[pltpu.VMEM((d, width), BF16)],
        compiler_params=_cparams("arbitrary"),
        name="norm_proj",
    )(xp, xs, meta, g, w)


def _proj_kernel(u_ref, w_ref, *refs, mode, scale, rider_blocks):
    w16 = refs[-1]
    n_rider = len(rider_blocks)
    riders_in, refs = refs[:n_rider], refs[n_rider:]

    @pl.when(pl.program_id(1) == 0)
    def _():
        w16[...] = w_ref[...].astype(BF16)

    riders_out = refs[len(refs) - 1 - n_rider:len(refs) - 1]
    for src, dst in zip(riders_in, riders_out):
        dst[...] = src[...].astype(BF16)

    acc = _dot(u_ref[...], w16[...])
    if mode == "f32":
        refs[0][...] = acc
    elif mode == "bf16":
        refs[0][...] = acc.astype(BF16)
    else:
        cos_ref, sin_ref, o_ref = refs[:3]
        cos, sin = cos_ref[...], sin_ref[...]
        for hh in range(acc.shape[1] // RET_DK):
            x = acc[:, hh * RET_DK:(hh + 1) * RET_DK]
            r = x * cos + pltpu.roll(x, RET_DK // 2, 1) * sin
            if scale is not None:
                r = r * scale
            o_ref[:, hh * RET_DK:(hh + 1) * RET_DK] = r.astype(o_ref.dtype)


def _project(u, w, col0, width, mode, tm, tn, rot=None, scale=None, out_dtype=None, riders=()):
    m, d = u.shape
    tn = min(tn, width)
    n_j, n_i = width // tn, m // tm
    j0 = col0 // tn
    w_mode = {"pipeline_mode": pl.Buffered(1)} if tn > COL_TILE else {}
    in_specs = [
        pl.BlockSpec((tm, d), lambda j, i: (i, 0)),
        pl.BlockSpec((d, tn), lambda j, i: (0, j0 + j), **w_mode),
    ]
    args = [u, w]
    rider_blocks, rider_specs = [], []
    for a in riders:
        nb = max(b for b in (1, 2, 4, 8, 16, 32, 64, 128) if b <= n_j * n_i and a.shape[0] % (16 * b) == 0)
        spec = pl.BlockSpec((a.shape[0] // nb, a.shape[1]), lambda j, i, nb=nb: (jnp.minimum(j * n_i + i, nb - 1), 0))
        rider_blocks.append(nb)
        rider_specs.append(spec)
    in_specs += rider_specs
    args += list(riders)
    out_specs = [pl.BlockSpec((tm, tn), lambda j, i: (i, j))]
    if mode == "rot":
        cos_t, sin_t, table_block = rot
        in_specs += [pl.BlockSpec((tm, RET_DK), lambda j, i: (table_block(i), 0))] * 2
        args += [cos_t, sin_t]
        out_shape = [jax.ShapeDtypeStruct((m, width), out_dtype)]
    else:
        out_shape = [jax.ShapeDtypeStruct((m, width), F32 if mode == "f32" else BF16)]
    out_specs += rider_specs
    out_shape += [jax.ShapeDtypeStruct(a.shape, BF16) for a in riders]
    res = pl.pallas_call(
        functools.partial(_proj_kernel, mode=mode, scale=scale, rider_blocks=tuple(rider_blocks)),
        grid=(n_j, n_i),
        in_specs=in_specs,
        out_specs=out_specs,
        out_shape=out_shape,
        scratch_shapes=[pltpu.VMEM((d, tn), BF16)],
        compiler_params=_cparams("arbitrary", "arbitrary"),
        name="proj_" + mode,
    )(*args)
    return res if riders else res[0]


def _proj_kv_kernel(u_ref, w_ref, o16_ref, outp_hbm, outs_hbm, w16, stage, sem,
                    *, tm, n_pt, n_st, tiles_per_batch, batch, rows_per_batch):
    i = pl.program_id(0)

    @pl.when(i == 0)
    def _():
        w16[...] = w_ref[...].astype(BF16)

    acc = _dot(u_ref[...], w16[...])
    o16_ref[...] = acc.astype(BF16)

    def tile_copy(dst):
        return pltpu.make_async_copy(stage, dst, sem)

    @pl.when(i > 0)
    def _():
        tile_copy(outp_hbm.at[pl.ds(0, tm)]).wait()

    for hd in range(HEADS):
        stage[:, hd, :] = acc[:, hd * SB_DH:(hd + 1) * SB_DH]

    @pl.when(i < n_pt)
    def _():
        b = i // tiles_per_batch
        r0 = (i % tiles_per_batch) * tm
        tile_copy(outp_hbm.at[pl.ds(b * rows_per_batch + N_META + r0, tm)]).start()

    @pl.when((i >= n_pt) & (i < n_pt + n_st))
    def _():
        tile_copy(outs_hbm.at[pl.ds((i - n_pt) * tm, tm)]).start()

    @pl.when(i == n_pt + n_st)
    def _():
        copies = [pltpu.make_async_copy(stage.at[pl.ds(0, N_META)],
                                        outp_hbm.at[pl.ds(b * rows_per_batch, N_META)], sem)
                  for b in range(batch)]
        for c in copies:
            c.start()
        for c in copies:
            c.wait()


def _project_kv(u, w, col0, tm, batch, seq, n_s):
    m, d = u.shape
    width = HEADS * SB_DH
    n_pt, n_st = batch * seq // tm, n_s // tm
    assert seq % tm == 0 and m // tm == n_pt + n_st + 1
    j0 = col0 // width
    rows_per_batch = N_META + seq
    return pl.pallas_call(
        functools.partial(_proj_kv_kernel, tm=tm, n_pt=n_pt, n_st=n_st, tiles_per_batch=seq // tm,
                          batch=batch, rows_per_batch=rows_per_batch),
        grid=(m // tm,),
        in_specs=[
            pl.BlockSpec((tm, d), lambda i: (i, 0)),
            pl.BlockSpec((d, width), lambda i: (0, j0), pipeline_mode=pl.Buffered(1)),
        ],
        out_specs=[
            pl.BlockSpec((tm, width), lambda i: (i, 0)),
            pl.BlockSpec(memory_space=pl.ANY),
            pl.BlockSpec(memory_space=pl.ANY),
        ],
        out_shape=[
            jax.ShapeDtypeStruct((m, width), BF16),
            jax.ShapeDtypeStruct((batch * rows_per_batch, HEADS, SB_DH), F32),
            jax.ShapeDtypeStruct((n_s, HEADS, SB_DH), F32),
        ],
        scratch_shapes=[
            pltpu.VMEM((d, width), BF16),
            pltpu.VMEM((tm, HEADS, SB_DH), F32),
            pltpu.SemaphoreType.DMA(()),
        ],
        compiler_params=_cparams("arbitrary"),
        name="proj_kv",
    )(u, w)


def _upper_ones(n):
    r = lax.broadcasted_iota(jnp.int32, (n, n), 0)
    c = lax.broadcasted_iota(jnp.int32, (n, n), 1)
    return (r > c).astype(BF16)


def _sb_tiles(tiles, tri):
    n = range(len(tiles))
    q, k, v, carry, acc, valid = zip(*tiles)
    z = [_dot_nt(q[i], k[i]) * (SB_DH ** -0.5) for i in n]
    e = [jnp.exp(-jnp.abs(z[i])) for i in n]
    sp = [jnp.maximum(z[i], 0.0) + jnp.log(1.0 + e[i]) for i in n]
    lf = [-sp[i] if valid[i] is None else jnp.where(valid[i], -sp[i], 0.0) for i in n]
    hi = [lf[i].astype(BF16) for i in n]
    lo = [(lf[i] - hi[i].astype(F32)).astype(BF16) for i in n]
    later = [_dot(hi[i], tri) + _dot(lo[i], tri) for i in n]
    a = [jnp.exp((z[i] - sp[i]) + (later[i] + carry[i])) for i in n]
    a = [a[i] if valid[i] is None else jnp.where(valid[i], a[i], 0.0) for i in n]
    acc = [acc[i] + _dot(a[i].astype(BF16), v[i]) for i in n]
    carry = [carry[i] + jnp.sum(lf[i], axis=-1, keepdims=True) for i in n]
    return list(zip(carry, acc))


def _sb_tile(q, k, v, carry, acc, tri, valid=None):
    return _sb_tiles([(q, k, v, carry, acc, valid)], tri)[0]


def _sb_prompt_kernel(q_ref, k_ref, v_ref, mk_ref, mv_ref, o_ref, *, t, n_q, unroll):
    tri = _upper_ones(t)
    tri2 = _upper_ones(2 * t)
    tri_m = _upper_ones(N_META)
    col_minus_row = (lax.broadcasted_iota(jnp.int32, (t, 2 * t), 1)
                     - lax.broadcasted_iota(jnp.int32, (t, 2 * t), 0))

    def window(i):
        r0 = pl.multiple_of(i * t, t)
        w0 = pl.multiple_of(jnp.maximum(i - 1, 0) * t, t)
        valid = col_minus_row < (r0 - w0)
        return (q_ref[pl.ds(r0, t), :], k_ref[pl.ds(w0, 2 * t), :], v_ref[pl.ds(w0, 2 * t), :],
                jnp.zeros((t, 1), F32), jnp.zeros((t, SB_DH), F32), valid)

    def older_keys(i, q, carry, acc):
        def cond(s):
            return (s[0] >= 0) & (jnp.max(s[1]) > SB_LOG_CUTOFF)

        def body(s):
            j, carry, acc = s
            c0 = pl.multiple_of(j * t, t)
            carry, acc = _sb_tile(q, k_ref[pl.ds(c0, t), :], v_ref[pl.ds(c0, t), :], carry, acc, tri)
            return j - 1, carry, acc

        _, carry, acc = lax.while_loop(cond, body, (i - 2, carry, acc))
        return lax.cond(
            jnp.max(carry) > SB_LOG_CUTOFF,
            lambda: _sb_tile(q, mk_ref[...], mv_ref[...], carry, acc, tri_m)[1],
            lambda: acc,
        )

    def group(p, _):
        tiles = [window(p * unroll + u) for u in range(unroll)]
        done = _sb_tiles(tiles, tri2)
        for u, (carry, acc) in enumerate(done):
            i = p * unroll + u
            q = tiles[u][0]
            acc = lax.cond(
                jnp.max(carry) > SB_LOG_CUTOFF,
                functools.partial(older_keys, i, q, carry, acc),
                lambda acc=acc: acc,
            )
            o_ref[pl.ds(pl.multiple_of(i * t, t), t), :] = acc.astype(o_ref.dtype)
        return 0

    lax.fori_loop(0, n_q // unroll, group, 0)


def _sb_prompt(q16, k16, v16, batch, seq, meta_row0, t):
    n_q = seq // t
    assert n_q >= 2
    unroll = max(u for u in (1, 2, 4, 8) if n_q % u == 0)
    mb = meta_row0 // N_META
    blk = lambda b, h: (b, h)
    mblk = lambda b, h: (mb, h)
    return pl.pallas_call(
        functools.partial(_sb_prompt_kernel, t=t, n_q=n_q, unroll=unroll),
        grid=(batch, HEADS),
        in_specs=[
            pl.BlockSpec((seq, SB_DH), blk),
            pl.BlockSpec((seq, SB_DH), blk),
            pl.BlockSpec((seq, SB_DH), blk),
            pl.BlockSpec((N_META, SB_DH), mblk),
            pl.BlockSpec((N_META, SB_DH), mblk),
        ],
        out_specs=pl.BlockSpec((seq, SB_DH), blk),
        out_shape=jax.ShapeDtypeStruct((batch * seq, HEADS * SB_DH), BF16),
        compiler_params=_cparams("arbitrary", "arbitrary"),
        name="sb_prompt",
    )(q16, k16, v16, k16, v16)


def _sb_sample_kernel(q_ref, k_ref, v_ref, ckl_ref, cvl_ref, ck_hbm, cv_hbm, o_ref, kbuf, vbuf, sem,
                      *, n, t, n_past):
    b = pl.program_id(0)
    tri_n = _upper_ones(n)
    tri = _upper_ones(t)
    causal = lax.broadcasted_iota(jnp.int32, (n, n), 1) < lax.broadcasted_iota(jnp.int32, (n, n), 0)

    def older_keys(hd, q, carry, acc):
        def cond(s):
            return (s[0] >= 0) & (jnp.max(s[1]) > SB_LOG_CUTOFF)

        def body(s):
            j, carry, acc = s
            rows = pl.ds(pl.multiple_of(j * t, t), t)
            ck = pltpu.make_async_copy(ck_hbm.at[0, b, rows], kbuf, sem.at[0])
            cv = pltpu.make_async_copy(cv_hbm.at[0, b, rows], vbuf, sem.at[1])
            ck.start()
            cv.start()
            ck.wait()
            cv.wait()
            carry, acc = _sb_tile(q, kbuf[:, hd, :].astype(BF16), vbuf[:, hd, :].astype(BF16), carry, acc, tri)
            return j - 1, carry, acc

        return lax.while_loop(cond, body, (n_past - 2, carry, acc))[2]

    cols = [slice(hd * SB_DH, (hd + 1) * SB_DH) for hd in range(HEADS)]
    qs = [q_ref[:, c] for c in cols]
    new = _sb_tiles([(qs[hd], k_ref[:, cols[hd]], v_ref[:, cols[hd]], jnp.zeros((n, 1), F32),
                      jnp.zeros((n, SB_DH), F32), causal) for hd in range(HEADS)], tri_n)
    old = _sb_tiles([(qs[hd], ckl_ref[0, 0, :, hd, :].astype(BF16), cvl_ref[0, 0, :, hd, :].astype(BF16),
                      new[hd][0], new[hd][1], None) for hd in range(HEADS)], tri)
    for hd, (carry, acc) in enumerate(old):
        q = qs[hd]
        acc = lax.cond(
            jnp.max(carry) > SB_LOG_CUTOFF,
            functools.partial(older_keys, hd, q, carry, acc),
            lambda acc=acc: acc,
        )
        o_ref[:, hd * SB_DH:(hd + 1) * SB_DH] = acc.astype(o_ref.dtype)


def _sb_sample(q16, k16, v16, cache_k, cache_v, n, row0, t):
    _, dec_b, past = cache_k.shape[:3]
    t = min(t, past)
    n_past = past // t
    rb = row0 // n
    w = HEADS * SB_DH
    blk = lambda b: (rb + b, 0)
    last = lambda b: (0, b, n_past - 1, 0, 0)
    return pl.pallas_call(
        functools.partial(_sb_sample_kernel, n=n, t=t, n_past=n_past),
        grid=(dec_b,),
        in_specs=[
            pl.BlockSpec((n, w), blk),
            pl.BlockSpec((n, w), blk),
            pl.BlockSpec((n, w), blk),
            pl.BlockSpec((1, 1, t, HEADS, SB_DH), last),
            pl.BlockSpec((1, 1, t, HEADS, SB_DH), last),
            pl.BlockSpec(memory_space=pl.ANY),
            pl.BlockSpec(memory_space=pl.ANY),
        ],
        out_specs=pl.BlockSpec((n, w), lambda b: (b, 0)),
        out_shape=jax.ShapeDtypeStruct((dec_b * n, w), BF16),
        scratch_shapes=[
            pltpu.VMEM((t, HEADS, SB_DH), F32),
            pltpu.VMEM((t, HEADS, SB_DH), F32),
            pltpu.SemaphoreType.DMA((2,)),
        ],
        compiler_params=_cparams("arbitrary"),
        name="sb_sample",
    )(q16, k16, v16, cache_k, cache_v, cache_k, cache_v)


def _ret_kernel(lg_ref, q_ref, k_ref, v_ref, g_ref, s0_ref, o_ref, s_out_ref, s_scr, d_scr, *, c, n_c):
    ci = pl.program_id(1)

    @pl.when((pl.program_id(0) == 0) & (ci == 0))
    def _():
        rel = (lax.broadcasted_iota(jnp.int32, (c, c), 0) - lax.broadcasted_iota(jnp.int32, (c, c), 1)).astype(F32)
        for hd in range(HEADS):
            d_scr[hd] = jnp.where(rel >= 0, jnp.exp(jnp.maximum(rel, 0.0) * lg_ref[hd]), 0.0)

    @pl.when(ci == 0)
    def _():
        s_scr[...] = s0_ref[0]

    tcol = lax.broadcasted_iota(jnp.int32, (c, 1), 0).astype(F32)
    for hd in range(HEADS):
        lg = lg_ref[hd]
        qk = slice(hd * RET_DK, (hd + 1) * RET_DK)
        vv = slice(hd * RET_DV, (hd + 1) * RET_DV)
        q = q_ref[:, qk]
        kf = k_ref[:, qk]
        v = v_ref[:, vv]
        s = s_scr[hd]
        scores = _dot_nt(q, kf.astype(BF16)) * d_scr[hd]
        o = _dot(scores.astype(BF16), v) + _dot(q, s.astype(BF16)) * jnp.exp((tcol + 1.0) * lg)
        k_dec = (kf * jnp.exp((c - 1.0 - tcol) * lg)).astype(BF16)
        s_scr[hd] = jnp.exp(c * lg) * s + _dot_tn(k_dec, v)
        o = o * lax.rsqrt(jnp.mean(o * o, axis=-1, keepdims=True) + EPS)
        g = g_ref[:, vv]
        o_ref[:, vv] = (o * (g * jax.nn.sigmoid(g))).astype(o_ref.dtype)

    @pl.when(ci == n_c - 1)
    def _():
        s_out_ref[0] = s_scr[...]


def _retention(log_g, rq16, rk32, rv16, gates, s0, batch, rows_per_batch, row0, c, shared_s0):
    n_c = rows_per_batch // c
    rb = row0 // c
    qkw, vw = HEADS * RET_DK, HEADS * RET_DV
    row = lambda b, ci: (rb + b * n_c + ci, 0)
    sidx = (lambda b, ci: (0, 0, 0, 0)) if shared_s0 else (lambda b, ci: (b, 0, 0, 0))
    return pl.pallas_call(
        functools.partial(_ret_kernel, c=c, n_c=n_c),
        grid=(batch, n_c),
        in_specs=[
            pl.BlockSpec(memory_space=pltpu.SMEM),
            pl.BlockSpec((c, qkw), row),
            pl.BlockSpec((c, qkw), row),
            pl.BlockSpec((c, vw), row),
            pl.BlockSpec((c, vw), row),
            pl.BlockSpec((1, HEADS, RET_DK, RET_DV), sidx),
        ],
        out_specs=[
            pl.BlockSpec((c, vw), lambda b, ci: (b * n_c + ci, 0)),
            pl.BlockSpec((1, HEADS, RET_DK, RET_DV), lambda b, ci: (b, 0, 0, 0)),
        ],
        out_shape=[
            jax.ShapeDtypeStruct((batch * rows_per_batch, vw), BF16),
            jax.ShapeDtypeStruct((batch, HEADS, RET_DK, RET_DV), F32),
        ],
        scratch_shapes=[pltpu.VMEM((HEADS, RET_DK, RET_DV), F32), pltpu.VMEM((HEADS, c, c), F32)],
        compiler_params=_cparams("arbitrary", "arbitrary"),
        name="retention",
    )(log_g, rq16, rk32, rv16, gates, s0)


def _route(logits):
    gl = [logits[r:r + 1, :] for r in range(N_GROUPS)]
    gmax = functools.reduce(jnp.maximum, gl)
    g_idx = jnp.where(gl[0] == gmax, 0, jnp.where(gl[1] == gmax, 1, jnp.where(gl[2] == gmax, 2, 3)))
    g_top = 1.0 / functools.reduce(lambda a, b: a + b, [jnp.exp(x - gmax) for x in gl])
    e = []
    for r in range(EXPERTS_PER_GROUP):
        rows = [logits[N_GROUPS + g * EXPERTS_PER_GROUP + r:N_GROUPS + g * EXPERTS_PER_GROUP + r + 1, :]
                for g in range(N_GROUPS)]
        e.append(jnp.where(g_idx == 0, rows[0], jnp.where(g_idx == 1, rows[1], jnp.where(g_idx == 2, rows[2], rows[3]))))
    m1 = functools.reduce(jnp.maximum, e)
    i1 = jnp.where(e[0] == m1, 0, jnp.where(e[1] == m1, 1, jnp.where(e[2] == m1, 2, 3)))
    e2 = [jnp.where(i1 == r, -jnp.inf, e[r]) for r in range(EXPERTS_PER_GROUP)]
    m2 = functools.reduce(jnp.maximum, e2)
    i2 = jnp.where(e2[0] == m2, 0, jnp.where(e2[1] == m2, 1, jnp.where(e2[2] == m2, 2, 3)))
    p2 = jnp.exp(m2 - m1)
    w1 = g_top / (1.0 + p2)
    w2 = g_top * p2 / (1.0 + p2)
    lo = jnp.minimum(i1, i2)
    hi = jnp.maximum(i1, i2)
    w_lo = jnp.where(i1 < i2, w1, w2)
    w_hi = jnp.where(i1 < i2, w2, w1)
    pair = jnp.where(lo == 0, hi - 1, jnp.where(lo == 1, hi + 1, 5))
    return g_idx * N_PAIRS + pair, w_lo, w_hi


def _gate_merge_kernel(osp_ref, oss_ref, orp_ref, ors_ref, gsb_ref, gret_ref, wsb_ref, wret_ref, m_ref, *, n_p):
    is_p = pl.program_id(1) < n_p
    o_sb = jnp.where(is_p, osp_ref[...], oss_ref[...])
    o_ret = jnp.where(is_p, orp_ref[...], ors_ref[...])
    merged = (jax.nn.sigmoid(gsb_ref[...]) * _dot(o_sb, wsb_ref[...])
              + jax.nn.sigmoid(gret_ref[...]) * _dot(o_ret, wret_ref[...]))
    m_ref[...] = merged.astype(m_ref.dtype)


def _gate_merge(osp, oss, orp, ors, gates, wsb, wret, tm, tn):
    d = wsb.shape[1]
    tn = min(tn, d)
    n_p, n_s = osp.shape[0] // tm, oss.shape[0] // tm
    n_j = d // tn
    vw = HEADS * RET_DV
    pidx = lambda j, i: (jnp.minimum(i, n_p - 1), 0)
    sidx = lambda j, i: (jnp.clip(i - n_p, 0, n_s - 1), 0)
    return pl.pallas_call(
        functools.partial(_gate_merge_kernel, n_p=n_p),
        grid=(n_j, n_p + n_s),
        in_specs=[
            pl.BlockSpec((tm, HEADS * SB_DH), pidx),
            pl.BlockSpec((tm, HEADS * SB_DH), sidx),
            pl.BlockSpec((tm, vw), pidx),
            pl.BlockSpec((tm, vw), sidx),
            pl.BlockSpec((tm, tn), lambda j, i: (i, vw // tn + j)),
            pl.BlockSpec((tm, tn), lambda j, i: (i, (vw + d) // tn + j)),
            pl.BlockSpec((wsb.shape[0], tn), lambda j, i: (0, j)),
            pl.BlockSpec((wret.shape[0], tn), lambda j, i: (0, j)),
        ],
        out_specs=pl.BlockSpec((tm, tn), lambda j, i: (i, j)),
        out_shape=jax.ShapeDtypeStruct(((n_p + n_s) * tm, d), BF16),
        compiler_params=_cparams("arbitrary", "arbitrary"),
        name="gate_merge",
    )(osp, oss, orp, ors, gates, gates, wsb, wret)


def _out_kernel(m_ref, xp_ref, xs_ref, wout_ref, n2g_ref, wrc_ref, br_ref, h_ref, route_ref, *, n_p):
    x = jnp.where(pl.program_id(0) < n_p, xp_ref[...], xs_ref[...])
    d = x.shape[1]
    h = x + _dot(m_ref[...], wout_ref[...])
    h_ref[:, :d] = h
    u2 = _rms(h, n2g_ref[...])
    u_hi = u2.astype(BF16)
    u_lo = (u2 - u_hi.astype(F32)).astype(BF16)
    hi_both = _dot(u_hi, wrc_ref[...])
    logits = (hi_both[:, :128] + (hi_both[:, 128:] + _dot(u_lo, wrc_ref[:, :128]))) + br_ref[...]
    cls, w_lo, w_hi = _route(logits.T)
    route_ref[...] = jnp.zeros_like(route_ref)
    route_ref[0:1, :] = cls.astype(F32)
    w_rows = jnp.concatenate([w_lo, w_hi, jnp.zeros((126, w_lo.shape[1]), F32)], axis=0)
    h_ref[:, d:] = w_rows.T


def _out_proj(m, xp, xs, wout, n2g, wr_cat, br, tm):
    d = xp.shape[1]
    n_p, n_s = xp.shape[0] // tm, xs.shape[0] // tm
    n = n_p + n_s
    const = lambda i: (0, 0)
    return pl.pallas_call(
        functools.partial(_out_kernel, n_p=n_p),
        grid=(n,),
        in_specs=[
            pl.BlockSpec((tm, d), lambda i: (i, 0)),
            pl.BlockSpec((tm, d), lambda i: (jnp.minimum(i, n_p - 1), 0)),
            pl.BlockSpec((tm, d), lambda i: (jnp.clip(i - n_p, 0, n_s - 1), 0)),
            pl.BlockSpec(wout.shape, const, pipeline_mode=pl.Buffered(1)),
            pl.BlockSpec((1, d), const),
            pl.BlockSpec(wr_cat.shape, const),
            pl.BlockSpec(br.shape, const),
        ],
        out_specs=[
            pl.BlockSpec((tm, d + 128), lambda i: (i, 0)),
            pl.BlockSpec((8, tm), lambda i: (0, i)),
        ],
        out_shape=[
            jax.ShapeDtypeStruct((n * tm, d + 128), F32),
            jax.ShapeDtypeStruct((8, n * tm), F32),
        ],
        compiler_params=_cparams("arbitrary"),
        name="out_proj",
    )(m, xp, xs, wout, n2g, wr_cat, br)


def _moe_kernel(elo_ref, ehi_ref, nvalid_ref, nprompt_ref, src_ref, nxt_ref, h_hbm, n2g_ref, nfg_ref,
                wg_lo, wu_lo, wd_lo, wg_hi, wu_hi, wd_hi, yp_hbm, ys_hbm, hbuf0, hbuf1, ybuf0, ybuf1, trash,
                sem_in, sem_out, *, tm, n_p):
    t = pl.program_id(0)
    n_valid = nvalid_ref[t]
    n_prompt = nprompt_ref[t]
    live = n_valid > 0
    prev_live = (t > 0) & (nvalid_ref[jnp.maximum(t - 1, 0)] > 0)
    hbuf, ybuf = (hbuf0, hbuf1), (ybuf0, ybuf1)
    d = ybuf0.shape[-1]

    def start_gather(idx_ref, s):
        for r in range(tm):
            pltpu.make_async_copy(h_hbm.at[pl.ds(idx_ref[0, 0, r], 1), :], hbuf[s].at[pl.ds(r, 1), :],
                                  sem_in.at[s]).start()

    def wait_gather(s):
        pltpu.make_async_copy(h_hbm.at[pl.ds(0, tm), :], hbuf[s], sem_in.at[s]).wait()

    def wait_scatter(s):
        pltpu.make_async_copy(ybuf[s], yp_hbm.at[pl.ds(0, tm), :], sem_out.at[s]).wait()

    @pl.when(t == 0)
    def _():
        start_gather(src_ref, 0)

    def tile(slot):
        wait_gather(slot)
        start_gather(nxt_ref, 1 - slot)
        hrows = hbuf[slot][:, :d]
        wts = hbuf[slot][:, d:]
        u = _rms(hrows, n2g_ref[...]).astype(BF16)
        y = jnp.zeros_like(hrows)
        for col, (wg, wu, wd) in enumerate(((wg_lo, wu_lo, wd_lo), (wg_hi, wu_hi, wd_hi))):
            gate = _dot(u, wg[0])
            hid = (gate * jax.nn.sigmoid(gate)) * _dot(u, wu[0])
            y = y + wts[:, col:col + 1] * _dot(hid.astype(BF16), wd[0])
        ybuf[slot][...] = _rms(hrows + y, nfg_ref[...])

        @pl.when(prev_live)
        def _():
            wait_scatter(1 - slot)

        def to_prompt(r):
            pltpu.make_async_copy(ybuf[slot].at[pl.ds(r, 1), :], yp_hbm.at[pl.ds(src_ref[0, 0, r], 1), :],
                                  sem_out.at[slot]).start()

        def to_sample(r):
            pltpu.make_async_copy(ybuf[slot].at[pl.ds(r, 1), :], ys_hbm.at[pl.ds(src_ref[0, 0, r] - n_p, 1), :],
                                  sem_out.at[slot]).start()

        def to_trash(r):
            pltpu.make_async_copy(ybuf[slot].at[pl.ds(r, 1), :], trash.at[pl.ds(r, 1), :],
                                  sem_out.at[slot]).start()

        def scatter_chunk(c, carry):
            r0 = c * SCATTER_CHUNK
            r1 = r0 + SCATTER_CHUNK
            all_prompt = r1 <= n_prompt
            all_sample = (r0 >= n_prompt) & (r1 <= n_valid)
            all_pad = r0 >= n_valid

            @pl.when(all_prompt)
            def _():
                for k in range(SCATTER_CHUNK):
                    to_prompt(r0 + k)

            @pl.when(all_sample)
            def _():
                for k in range(SCATTER_CHUNK):
                    to_sample(r0 + k)

            @pl.when(all_pad)
            def _():
                for k in range(SCATTER_CHUNK):
                    to_trash(r0 + k)

            @pl.when(jnp.logical_not(all_prompt | all_sample | all_pad))
            def _():
                for k in range(SCATTER_CHUNK):
                    r = r0 + k
                    pl.when(r < n_prompt)(functools.partial(to_prompt, r))
                    pl.when((r >= n_prompt) & (r < n_valid))(functools.partial(to_sample, r))
                    pl.when(r >= n_valid)(functools.partial(to_trash, r))

            return carry

        lax.fori_loop(0, tm // SCATTER_CHUNK, scatter_chunk, 0)

        @pl.when(nvalid_ref[t + 1] == 0)
        def _():
            wait_scatter(slot)

    for slot in (0, 1):
        parity = t % 2 == slot
        pl.when(prev_live & jnp.logical_not(live) & parity)(functools.partial(wait_gather, slot))
        pl.when(live & parity)(functools.partial(tile, slot))


def _moe(h_ext, elo, ehi, nvalid, nprompt, src, n2g, nfg, wg, wu, wd, tm, n_p):
    n, dx = h_ext.shape
    d = wg.shape[1]
    f = wg.shape[-1]
    n_t = src.shape[0]
    lo = lambda t, elo, ehi, nv, npr: (elo[t], 0, 0)
    hi = lambda t, elo, ehi, nv, npr: (ehi[t], 0, 0)
    const = lambda t, *_: (0, 0)
    grid_spec = pltpu.PrefetchScalarGridSpec(
        num_scalar_prefetch=4,
        grid=(n_t,),
        in_specs=[
            pl.BlockSpec((1, 1, tm), lambda t, *_: (t, 0, 0), memory_space=pltpu.SMEM),
            pl.BlockSpec((1, 1, tm), lambda t, *_: (jnp.minimum(t + 1, n_t - 1), 0, 0), memory_space=pltpu.SMEM),
            pl.BlockSpec(memory_space=pl.ANY),
            pl.BlockSpec((1, d), const),
            pl.BlockSpec((1, d), const),
            pl.BlockSpec((1, d, f), lo),
            pl.BlockSpec((1, d, f), lo),
            pl.BlockSpec((1, f, d), lo),
            pl.BlockSpec((1, d, f), hi),
            pl.BlockSpec((1, d, f), hi),
            pl.BlockSpec((1, f, d), hi),
        ],
        out_specs=[pl.BlockSpec(memory_space=pl.ANY), pl.BlockSpec(memory_space=pl.ANY)],
        scratch_shapes=[
            pltpu.VMEM((tm, dx), F32),
            pltpu.VMEM((tm, dx), F32),
            pltpu.VMEM((tm, d), F32),
            pltpu.VMEM((tm, d), F32),
            pltpu.VMEM((tm, d), F32),
            pltpu.SemaphoreType.DMA((2,)),
            pltpu.SemaphoreType.DMA((2,)),
        ],
    )
    return pl.pallas_call(
        functools.partial(_moe_kernel, tm=tm, n_p=n_p),
        grid_spec=grid_spec,
        out_shape=[jax.ShapeDtypeStruct((n_p, d), F32), jax.ShapeDtypeStruct((n - n_p, d), F32)],
        compiler_params=_cparams("arbitrary"),
        name="moe",
    )(elo, ehi, nvalid, nprompt, src, src, h_ext, n2g, nfg, wg, wu, wd, wg, wu, wd)


def _sort_by_class(route, tm, n_p):
    n = route.shape[1]
    n_t = n // tm + N_CLASSES + 1
    cls = route[0].astype(jnp.int32)
    onehot = (cls[:, None] == jnp.arange(N_CLASSES, dtype=jnp.int32)[None, :]).astype(jnp.int32)
    rank = jnp.sum((jnp.cumsum(onehot, axis=0) - onehot) * onehot, axis=1)
    tiles_per = (jnp.sum(onehot, axis=0) + tm - 1) // tm
    tile_end = jnp.cumsum(tiles_per)
    nused = tile_end[-1]
    pos = (tile_end - tiles_per)[cls] * tm + rank
    tile_ids = jnp.minimum(jnp.arange(n_t, dtype=jnp.int32), nused - 1)
    tile_cls = jnp.minimum(jnp.sum((tile_ids[:, None] >= tile_end[None, :]).astype(jnp.int32), axis=1), N_CLASSES - 1)
    pair_lo = jnp.array([0, 0, 0, 1, 1, 2], jnp.int32)
    pair_hi = jnp.array([1, 2, 3, 2, 3, 3], jnp.int32)
    elo = (tile_cls // N_PAIRS) * EXPERTS_PER_GROUP + pair_lo[tile_cls % N_PAIRS]
    ehi = (tile_cls // N_PAIRS) * EXPERTS_PER_GROUP + pair_hi[tile_cls % N_PAIRS]
    p = n_t * tm
    src = jnp.full((p,), -1, jnp.int32).at[pos].set(jnp.arange(n, dtype=jnp.int32))
    nvalid = jnp.sum((src >= 0).reshape(n_t, tm), axis=1).astype(jnp.int32)
    nprompt = jnp.sum(((src >= 0) & (src < n_p)).reshape(n_t, tm), axis=1).astype(jnp.int32)
    src = jnp.maximum(src, 0)
    return elo, ehi, nvalid, nprompt, src.reshape(n_t, 1, tm)


def kernel(x_prompt, x_sample, cache_sb_k, cache_sb_v, state_ret, meta, norm1_g, w_in, w_sb_o, w_ret_o,
           w_out, norm2_g, w_grp, b_grp, w_exp, b_exp, w_gate, w_up, w_down, normf_g):
    batch, seq, d = x_prompt.shape
    dec_b, dec_n, _ = x_sample.shape
    depth, _, past = cache_sb_k.shape[:3]
    assert depth == 1
    n_p, n_s = batch * seq, dec_b * dec_n
    sbw, qkw, vw = HEADS * SB_DH, HEADS * RET_DK, HEADS * RET_DV
    tm = min(ROW_TILE, n_s)
    tm_norm = min(NORM_TILE, n_s)
    assert tm % tm_norm == 0 and n_p % tm == 0 and n_s % tm == 0

    xp = x_prompt.reshape(n_p, d)
    xs = x_sample.reshape(n_s, d)
    u, sb_q16 = _norm_and_project(xp, xs, meta, norm1_g, w_in[0], sbw, tm_norm, tm // tm_norm)
    meta_row0 = n_p + n_s

    assert seq % tm == 0 and tm % dec_n == 0
    pos = jnp.concatenate([
        jnp.arange(seq, dtype=F32),
        jnp.tile(past + jnp.arange(dec_n, dtype=F32), tm // dec_n),
        jnp.arange(tm, dtype=F32) - N_META,
    ])
    inv_freq = 1.0 / (ROPE_BASE ** (jnp.arange(0, RET_DK, 2, dtype=F32) / RET_DK))
    ang = pos[:, None] * inv_freq[None, :]
    cos_t = jnp.concatenate([jnp.cos(ang), jnp.cos(ang)], axis=1)
    sin_t = jnp.concatenate([-jnp.sin(ang), jnp.sin(ang)], axis=1)
    seq_tiles, n_pt, n_st = seq // tm, n_p // tm, n_s // tm

    def table_block(i):
        return jnp.where(i < n_pt, i % seq_tiles, jnp.where(i < n_pt + n_st, seq_tiles, seq_tiles + 1))

    proj = functools.partial(_project, u, w_in[0], tm=tm, tn=COL_TILE)
    c = sbw
    sb_k16, k_prompt, k_sample = _project_kv(u, w_in[0], c, tm, batch, seq, n_s); c += sbw
    sb_v16, v_prompt, v_sample = _project_kv(u, w_in[0], c, tm, batch, seq, n_s); c += sbw
    rot = (cos_t, sin_t, table_block)
    r_q16 = proj(c, qkw, "rot", rot=rot, out_dtype=BF16); c += qkw
    r_k32 = proj(c, qkw, "rot", rot=rot, scale=RET_DK ** -0.5, out_dtype=F32); c += qkw
    r_v16 = proj(c, vw, "bf16"); c += vw
    n_exp, _, d_exp = w_gate.shape[1:]
    gates, wg16, wu16, wd16, wsb16, wret16, wout16 = proj(
        c, vw + 2 * d, "f32",
        riders=(w_gate.reshape(n_exp * d, d_exp), w_up.reshape(n_exp * d, d_exp), w_down.reshape(n_exp * d_exp, d),
                w_sb_o[0], w_ret_o[0], w_out[0]))
    wg16 = wg16.reshape(n_exp, d, d_exp)
    wu16 = wu16.reshape(n_exp, d, d_exp)
    wd16 = wd16.reshape(n_exp, d_exp, d)

    o_sb_p = _sb_prompt(sb_q16, sb_k16, sb_v16, batch, seq, meta_row0, min(SB_TILE, seq))
    o_sb_s = _sb_sample(sb_q16, sb_k16, sb_v16, cache_sb_k, cache_sb_v, dec_n, n_p, SB_TILE)

    log_g = jnp.log1p(-jnp.power(2.0, -5.0 - jnp.arange(HEADS, dtype=F32)))
    ret = functools.partial(_retention, log_g, r_q16, r_k32, r_v16, gates)
    zero_state = jnp.zeros((1, HEADS, RET_DK, RET_DV), F32)
    _, s_meta = ret(zero_state, 1, N_META, meta_row0, N_META, True)
    o_ret_p, s_p = ret(s_meta, batch, seq, 0, min(RET_CHUNK, seq), True)
    o_ret_s, s_s = ret(state_ret[0], dec_b, dec_n, n_p, dec_n, False)

    n_route = N_GROUPS + N_GROUPS * EXPERTS_PER_GROUP
    w_router = jnp.pad(jnp.concatenate([w_grp[0], w_exp[0]], axis=1), ((0, 0), (0, 128 - n_route)))
    b_router = jnp.pad(jnp.concatenate([b_grp[0], b_exp[0]]), (0, 128 - n_route)).reshape(1, 128)
    tmm = min(MERGE_TILE, n_s)
    merged = _gate_merge(o_sb_p, o_sb_s, o_ret_p, o_ret_s, gates,
                         wsb16, wret16, tmm, COL_TILE)
    wr_hi = w_router.astype(BF16)
    wr_lo = (w_router - wr_hi.astype(F32)).astype(BF16)
    wr_cat = jnp.concatenate([wr_hi, wr_lo], axis=1)
    h, route = _out_proj(merged, xp, xs, wout16, norm2_g, wr_cat, b_router, tmm)

    tmo = min(MOE_TILE, n_s)
    elo, ehi, nvalid, nprompt, src = _sort_by_class(route, tmo, n_p)
    y_p, y_s = _moe(h, elo, ehi, nvalid, nprompt, src, norm2_g, normf_g.reshape(1, d),
                    wg16, wu16, wd16, tmo, n_p)
    y_prompt = y_p.reshape(batch, seq, d)
    y_sample = y_s.reshape(dec_b, dec_n, d)

    pshape = (1, batch, N_META + seq, HEADS, SB_DH)
    sshape = (1, dec_b, dec_n, HEADS, SB_DH)
    return (y_prompt, y_sample, k_prompt.reshape(pshape), v_prompt.reshape(pshape), s_p[None],
            k_sample.reshape(sshape), v_sample.reshape(sshape), s_s[None])
```

```python
import functools

import jax
import jax.numpy as jnp
from jax import lax
from jax.experimental import pallas as pl
from jax.experimental.pallas import tpu as pltpu

F32 = jnp.float32
BF16 = jnp.bfloat16

N_META = 16
HEADS = 8
SB_DH = 128
RET_DK = 128
RET_DV = 256
N_GROUPS = 4
EXPERTS_PER_GROUP = 4
N_PAIRS = 6
N_CLASSES = N_GROUPS * N_PAIRS
ROPE_BASE = 10000.0
EPS = 1e-6
SB_LOG_CUTOFF = -88.0

ROW_TILE = 1024
NORM_TILE = 512
COL_TILE = 1024
MERGE_TILE = 512
MOE_TILE = 256
SCATTER_CHUNK = 16
SB_TILE = 128
RET_CHUNK = 256
VMEM_LIMIT = 56 * 1024 * 1024


def _cparams(*sem):
    return pltpu.CompilerParams(dimension_semantics=sem, vmem_limit_bytes=VMEM_LIMIT)


def _rms(x, g):
    return (x * lax.rsqrt(jnp.mean(x * x, axis=-1, keepdims=True) + EPS)) * g


def _dot(a, b):
    return jnp.dot(a, b, preferred_element_type=F32)


def _dot_nt(a, b):
    return lax.dot_general(a, b, (((1,), (1,)), ((), ())), preferred_element_type=F32)


def _dot_tn(a, b):
    return lax.dot_general(a, b, (((0,), (0,)), ((), ())), preferred_element_type=F32)


def _norm_proj_kernel(xp_ref, xs_ref, meta_ref, g_ref, w_ref, u_ref, o_ref, w16, *, n_p, n_s):
    i = pl.program_id(0)

    @pl.when(i == 0)
    def _():
        w16[...] = w_ref[...].astype(BF16)

    @pl.when(i < n_p)
    def _():
        u_ref[...] = _rms(xp_ref[...], g_ref[...]).astype(BF16)

    @pl.when((i >= n_p) & (i < n_p + n_s))
    def _():
        u_ref[...] = _rms(xs_ref[...], g_ref[...]).astype(BF16)

    @pl.when(i >= n_p + n_s)
    def _():
        u_ref[...] = jnp.zeros_like(u_ref)

    @pl.when(i == n_p + n_s)
    def _():
        u_ref[0:N_META, :] = _rms(meta_ref[...], g_ref[...]).astype(BF16)

    o_ref[...] = _dot(u_ref[...], w16[...]).astype(BF16)


def _norm_and_project(xp, xs, meta, g, w, width, tm, tail_tiles):
    d = xp.shape[1]
    n_p, n_s = xp.shape[0] // tm, xs.shape[0] // tm
    m = (n_p + n_s + tail_tiles) * tm
    return pl.pallas_call(
        functools.partial(_norm_proj_kernel, n_p=n_p, n_s=n_s),
        grid=(n_p + n_s + tail_tiles,),
        in_specs=[
            pl.BlockSpec((tm, d), lambda i: (jnp.minimum(i, n_p - 1), 0)),
            pl.BlockSpec((tm, d), lambda i: (jnp.clip(i - n_p, 0, n_s - 1), 0)),
            pl.BlockSpec((N_META, d), lambda i: (0, 0)),
            pl.BlockSpec((1, d), lambda i: (0, 0)),
            pl.BlockSpec((d, width), lambda i: (0, 0), pipeline_mode=pl.Buffered(1)),
        ],
        out_specs=[pl.BlockSpec((tm, d), lambda i: (i, 0)), pl.BlockSpec((tm, width), lambda i: (i, 0))],
        out_shape=[jax.ShapeDtypeStruct((m, d), BF16), jax.ShapeDtypeStruct((m, width), BF16)],
        scratch_shapes=[pltpu.VMEM((d, width), BF16)],
        compiler_params=_cparams("arbitrary"),
        name="norm_proj",
    )(xp, xs, meta, g, w)


def _proj_kernel(u_ref, w_ref, *refs, mode, scale, rider_blocks):
    w16 = refs[-1]
    n_rider = len(rider_blocks)
    riders_in, refs = refs[:n_rider], refs[n_rider:]

    @pl.when(pl.program_id(1) == 0)
    def _():
        w16[...] = w_ref[...].astype(BF16)

    riders_out = refs[len(refs) - 1 - n_rider:len(refs) - 1]
    for src, dst in zip(riders_in, riders_out):
        dst[...] = src[...].astype(BF16)

    acc = _dot(u_ref[...], w16[...])
    if mode == "f32":
        refs[0][...] = acc
    elif mode == "bf16":
        refs[0][...] = acc.astype(BF16)
    else:
        cos_ref, sin_ref, o_ref = refs[:3]
        cos, sin = cos_ref[...], sin_ref[...]
        for hh in range(acc.shape[1] // RET_DK):
            x = acc[:, hh * RET_DK:(hh + 1) * RET_DK]
            r = x * cos + pltpu.roll(x, RET_DK // 2, 1) * sin
            if scale is not None:
                r = r * scale
            o_ref[:, hh * RET_DK:(hh + 1) * RET_DK] = r.astype(o_ref.dtype)


def _project(u, w, col0, width, mode, tm, tn, rot=None, scale=None, out_dtype=None, riders=()):
    m, d = u.shape
    tn = min(tn, width)
    n_j, n_i = width // tn, m // tm
    j0 = col0 // tn
    w_mode = {"pipeline_mode": pl.Buffered(1)} if tn > COL_TILE else {}
    in_specs = [
        pl.BlockSpec((tm, d), lambda j, i: (i, 0)),
        pl.BlockSpec((d, tn), lambda j, i: (0, j0 + j), **w_mode),
    ]
    args = [u, w]
    rider_blocks, rider_specs = [], []
    for a in riders:
        nb = max(b for b in (1, 2, 4, 8, 16, 32, 64, 128) if b <= n_j * n_i and a.shape[0] % (16 * b) == 0)
        spec = pl.BlockSpec((a.shape[0] // nb, a.shape[1]), lambda j, i, nb=nb: (jnp.minimum(j * n_i + i, nb - 1), 0))
        rider_blocks.append(nb)
        rider_specs.append(spec)
    in_specs += rider_specs
    args += list(riders)
    out_specs = [pl.BlockSpec((tm, tn), lambda j, i: (i, j))]
    if mode == "rot":
        cos_t, sin_t, table_block = rot
        in_specs += [pl.BlockSpec((tm, RET_DK), lambda j, i: (table_block(i), 0))] * 2
        args += [cos_t, sin_t]
        out_shape = [jax.ShapeDtypeStruct((m, width), out_dtype)]
    else:
        out_shape = [jax.ShapeDtypeStruct((m, width), F32 if mode == "f32" else BF16)]
    out_specs += rider_specs
    out_shape += [jax.ShapeDtypeStruct(a.shape, BF16) for a in riders]
    res = pl.pallas_call(
        functools.partial(_proj_kernel, mode=mode, scale=scale, rider_blocks=tuple(rider_blocks)),
        grid=(n_j, n_i),
        in_specs=in_specs,
        out_specs=out_specs,
        out_shape=out_shape,
        scratch_shapes=[pltpu.VMEM((d, tn), BF16)],
        compiler_params=_cparams("arbitrary", "arbitrary"),
        name="proj_" + mode,
    )(*args)
    return res if riders else res[0]


def _proj_kv_kernel(u_ref, w_ref, o16_ref, outp_hbm, outs_hbm, w16, stage, sem,
                    *, tm, n_pt, n_st, tiles_per_batch, batch, rows_per_batch):
    i = pl.program_id(0)

    @pl.when(i == 0)
    def _():
        w16[...] = w_ref[...].astype(BF16)

    acc = _dot(u_ref[...], w16[...])
    o16_ref[...] = acc.astype(BF16)

    def tile_copy(dst):
        return pltpu.make_async_copy(stage, dst, sem)

    @pl.when(i > 0)
    def _():
        tile_copy(outp_hbm.at[pl.ds(0, tm)]).wait()

    stage[...] = acc.reshape(tm, HEADS, SB_DH)

    @pl.when(i < n_pt)
    def _():
        b = i // tiles_per_batch
        r0 = (i % tiles_per_batch) * tm
        tile_copy(outp_hbm.at[pl.ds(b * rows_per_batch + N_META + r0, tm)]).start()

    @pl.when((i >= n_pt) & (i < n_pt + n_st))
    def _():
        tile_copy(outs_hbm.at[pl.ds((i - n_pt) * tm, tm)]).start()

    @pl.when(i == n_pt + n_st)
    def _():
        copies = [pltpu.make_async_copy(stage.at[pl.ds(0, N_META)],
                                        outp_hbm.at[pl.ds(b * rows_per_batch, N_META)], sem)
                  for b in range(batch)]
        for c in copies:
            c.start()
        for c in copies:
            c.wait()


def _project_kv(u, w, col0, tm, batch, seq, n_s):
    m, d = u.shape
    width = HEADS * SB_DH
    n_pt, n_st = batch * seq // tm, n_s // tm
    assert seq % tm == 0 and m // tm == n_pt + n_st + 1
    j0 = col0 // width
    rows_per_batch = N_META + seq
    return pl.pallas_call(
        functools.partial(_proj_kv_kernel, tm=tm, n_pt=n_pt, n_st=n_st, tiles_per_batch=seq // tm,
                          batch=batch, rows_per_batch=rows_per_batch),
        grid=(m // tm,),
        in_specs=[
            pl.BlockSpec((tm, d), lambda i: (i, 0)),
            pl.BlockSpec((d, width), lambda i: (0, j0), pipeline_mode=pl.Buffered(1)),
        ],
        out_specs=[
            pl.BlockSpec((tm, width), lambda i: (i, 0)),
            pl.BlockSpec(memory_space=pl.ANY),
            pl.BlockSpec(memory_space=pl.ANY),
        ],
        out_shape=[
            jax.ShapeDtypeStruct((m, width), BF16),
            jax.ShapeDtypeStruct((batch * rows_per_batch, HEADS, SB_DH), F32),
            jax.ShapeDtypeStruct((n_s, HEADS, SB_DH), F32),
        ],
        scratch_shapes=[
            pltpu.VMEM((d, width), BF16),
            pltpu.VMEM((tm, HEADS, SB_DH), F32),
            pltpu.SemaphoreType.DMA(()),
        ],
        compiler_params=_cparams("arbitrary"),
        name="proj_kv",
    )(u, w)


def _upper_ones(n):
    r = lax.broadcasted_iota(jnp.int32, (n, n), 0)
    c = lax.broadcasted_iota(jnp.int32, (n, n), 1)
    return (r > c).astype(BF16)


def _sb_tiles(tiles, tri):
    n = range(len(tiles))
    q, k, v, carry, acc, valid = zip(*tiles)
    z = [_dot_nt(q[i], k[i]) * (SB_DH ** -0.5) for i in n]
    e = [jnp.exp(-jnp.abs(z[i])) for i in n]
    sp = [jnp.maximum(z[i], 0.0) + jnp.log(1.0 + e[i]) for i in n]
    lf = [-sp[i] if valid[i] is None else jnp.where(valid[i], -sp[i], 0.0) for i in n]
    hi = [lf[i].astype(BF16) for i in n]
    lo = [(lf[i] - hi[i].astype(F32)).astype(BF16) for i in n]
    later = [_dot(hi[i], tri) + _dot(lo[i], tri) for i in n]
    a = [jnp.exp((z[i] - sp[i]) + (later[i] + carry[i])) for i in n]
    a = [a[i] if valid[i] is None else jnp.where(valid[i], a[i], 0.0) for i in n]
    acc = [acc[i] + _dot(a[i].astype(BF16), v[i]) for i in n]
    carry = [carry[i] + jnp.sum(lf[i], axis=-1, keepdims=True) for i in n]
    return list(zip(carry, acc))


def _sb_tile(q, k, v, carry, acc, tri, valid=None):
    return _sb_tiles([(q, k, v, carry, acc, valid)], tri)[0]


def _sb_prompt_kernel(q_ref, k_ref, v_ref, mk_ref, mv_ref, o_ref, *, t, n_q, unroll):
    tri = _upper_ones(t)
    tri2 = _upper_ones(2 * t)
    tri_m = _upper_ones(N_META)
    col_minus_row = (lax.broadcasted_iota(jnp.int32, (t, 2 * t), 1)
                     - lax.broadcasted_iota(jnp.int32, (t, 2 * t), 0))

    def window(i):
        r0 = pl.multiple_of(i * t, t)
        w0 = pl.multiple_of(jnp.maximum(i - 1, 0) * t, t)
        valid = col_minus_row < (r0 - w0)
        return (q_ref[pl.ds(r0, t), :], k_ref[pl.ds(w0, 2 * t), :], v_ref[pl.ds(w0, 2 * t), :],
                jnp.zeros((t, 1), F32), jnp.zeros((t, SB_DH), F32), valid)

    def older_keys(i, q, carry, acc):
        def cond(s):
            return (s[0] >= 0) & (jnp.max(s[1]) > SB_LOG_CUTOFF)

        def body(s):
            j, carry, acc = s
            c0 = pl.multiple_of(j * t, t)
            carry, acc = _sb_tile(q, k_ref[pl.ds(c0, t), :], v_ref[pl.ds(c0, t), :], carry, acc, tri)
            return j - 1, carry, acc

        _, carry, acc = lax.while_loop(cond, body, (i - 2, carry, acc))
        return lax.cond(
            jnp.max(carry) > SB_LOG_CUTOFF,
            lambda: _sb_tile(q, mk_ref[...], mv_ref[...], carry, acc, tri_m)[1],
            lambda: acc,
        )

    def group(p, _):
        tiles = [window(p * unroll + u) for u in range(unroll)]
        done = _sb_tiles(tiles, tri2)
        for u, (carry, acc) in enumerate(done):
            i = p * unroll + u
            q = tiles[u][0]
            acc = lax.cond(
                jnp.max(carry) > SB_LOG_CUTOFF,
                functools.partial(older_keys, i, q, carry, acc),
                lambda acc=acc: acc,
            )
            o_ref[pl.ds(pl.multiple_of(i * t, t), t), :] = acc.astype(o_ref.dtype)
        return 0

    lax.fori_loop(0, n_q // unroll, group, 0)


def _sb_prompt(q16, k16, v16, batch, seq, meta_row0, t):
    n_q = seq // t
    assert n_q >= 2
    unroll = max(u for u in (1, 2, 4, 8) if n_q % u == 0)
    mb = meta_row0 // N_META
    blk = lambda b, h: (b, h)
    mblk = lambda b, h: (mb, h)
    return pl.pallas_call(
        functools.partial(_sb_prompt_kernel, t=t, n_q=n_q, unroll=unroll),
        grid=(batch, HEADS),
        in_specs=[
            pl.BlockSpec((seq, SB_DH), blk),
            pl.BlockSpec((seq, SB_DH), blk),
            pl.BlockSpec((seq, SB_DH), blk),
            pl.BlockSpec((N_META, SB_DH), mblk),
            pl.BlockSpec((N_META, SB_DH), mblk),
        ],
        out_specs=pl.BlockSpec((seq, SB_DH), blk),
        out_shape=jax.ShapeDtypeStruct((batch * seq, HEADS * SB_DH), BF16),
        compiler_params=_cparams("arbitrary", "arbitrary"),
        name="sb_prompt",
    )(q16, k16, v16, k16, v16)


def _sb_sample_kernel(q_ref, k_ref, v_ref, ckl_ref, cvl_ref, ck_hbm, cv_hbm, o_ref, kbuf, vbuf, sem,
                      *, n, t, n_past):
    b = pl.program_id(0)
    tri_n = _upper_ones(n)
    tri = _upper_ones(t)
    causal = lax.broadcasted_iota(jnp.int32, (n, n), 1) < lax.broadcasted_iota(jnp.int32, (n, n), 0)

    def older_keys(hd, q, carry, acc):
        def cond(s):
            return (s[0] >= 0) & (jnp.max(s[1]) > SB_LOG_CUTOFF)

        def body(s):
            j, carry, acc = s
            rows = pl.ds(pl.multiple_of(j * t, t), t)
            ck = pltpu.make_async_copy(ck_hbm.at[0, b, rows], kbuf, sem.at[0])
            cv = pltpu.make_async_copy(cv_hbm.at[0, b, rows], vbuf, sem.at[1])
            ck.start()
            cv.start()
            ck.wait()
            cv.wait()
            carry, acc = _sb_tile(q, kbuf[:, hd, :].astype(BF16), vbuf[:, hd, :].astype(BF16), carry, acc, tri)
            return j - 1, carry, acc

        return lax.while_loop(cond, body, (n_past - 2, carry, acc))[2]

    cols = [slice(hd * SB_DH, (hd + 1) * SB_DH) for hd in range(HEADS)]
    qs = [q_ref[:, c] for c in cols]
    new = _sb_tiles([(qs[hd], k_ref[:, cols[hd]], v_ref[:, cols[hd]], jnp.zeros((n, 1), F32),
                      jnp.zeros((n, SB_DH), F32), causal) for hd in range(HEADS)], tri_n)
    old = _sb_tiles([(qs[hd], ckl_ref[0, 0, :, hd, :].astype(BF16), cvl_ref[0, 0, :, hd, :].astype(BF16),
                      new[hd][0], new[hd][1], None) for hd in range(HEADS)], tri)
    for hd, (carry, acc) in enumerate(old):
        q = qs[hd]
        acc = lax.cond(
            jnp.max(carry) > SB_LOG_CUTOFF,
            functools.partial(older_keys, hd, q, carry, acc),
            lambda acc=acc: acc,
        )
        o_ref[:, hd * SB_DH:(hd + 1) * SB_DH] = acc.astype(o_ref.dtype)


def _sb_sample(q16, k16, v16, cache_k, cache_v, n, row0, t):
    _, dec_b, past = cache_k.shape[:3]
    t = min(t, past)
    n_past = past // t
    rb = row0 // n
    w = HEADS * SB_DH
    blk = lambda b: (rb + b, 0)
    last = lambda b: (0, b, n_past - 1, 0, 0)
    return pl.pallas_call(
        functools.partial(_sb_sample_kernel, n=n, t=t, n_past=n_past),
        grid=(dec_b,),
        in_specs=[
            pl.BlockSpec((n, w), blk),
            pl.BlockSpec((n, w), blk),
            pl.BlockSpec((n, w), blk),
            pl.BlockSpec((1, 1, t, HEADS, SB_DH), last),
            pl.BlockSpec((1, 1, t, HEADS, SB_DH), last),
            pl.BlockSpec(memory_space=pl.ANY),
            pl.BlockSpec(memory_space=pl.ANY),
        ],
        out_specs=pl.BlockSpec((n, w), lambda b: (b, 0)),
        out_shape=jax.ShapeDtypeStruct((dec_b * n, w), BF16),
        scratch_shapes=[
            pltpu.VMEM((t, HEADS, SB_DH), F32),
            pltpu.VMEM((t, HEADS, SB_DH), F32),
            pltpu.SemaphoreType.DMA((2,)),
        ],
        compiler_params=_cparams("arbitrary"),
        name="sb_sample",
    )(q16, k16, v16, cache_k, cache_v, cache_k, cache_v)


def _ret_kernel(lg_ref, q_ref, k_ref, v_ref, g_ref, s0_ref, o_ref, s_out_ref, s_scr, d_scr, *, c, n_c):
    ci = pl.program_id(1)

    @pl.when((pl.program_id(0) == 0) & (ci == 0))
    def _():
        rel = (lax.broadcasted_iota(jnp.int32, (c, c), 0) - lax.broadcasted_iota(jnp.int32, (c, c), 1)).astype(F32)
        for hd in range(HEADS):
            d_scr[hd] = jnp.where(rel >= 0, jnp.exp(jnp.maximum(rel, 0.0) * lg_ref[hd]), 0.0)

    @pl.when(ci == 0)
    def _():
        s_scr[...] = s0_ref[0]

    tcol = lax.broadcasted_iota(jnp.int32, (c, 1), 0).astype(F32)
    for hd in range(HEADS):
        lg = lg_ref[hd]
        qk = slice(hd * RET_DK, (hd + 1) * RET_DK)
        vv = slice(hd * RET_DV, (hd + 1) * RET_DV)
        q = q_ref[:, qk]
        kf = k_ref[:, qk]
        v = v_ref[:, vv]
        s = s_scr[hd]
        scores = _dot_nt(q, kf.astype(BF16)) * d_scr[hd]
        o = _dot(scores.astype(BF16), v) + _dot(q, s.astype(BF16)) * jnp.exp((tcol + 1.0) * lg)
        k_dec = (kf * jnp.exp((c - 1.0 - tcol) * lg)).astype(BF16)
        s_scr[hd] = jnp.exp(c * lg) * s + _dot_tn(k_dec, v)
        o = o * lax.rsqrt(jnp.mean(o * o, axis=-1, keepdims=True) + EPS)
        g = g_ref[:, vv]
        o_ref[:, vv] = (o * (g * jax.nn.sigmoid(g))).astype(o_ref.dtype)

    @pl.when(ci == n_c - 1)
    def _():
        s_out_ref[0] = s_scr[...]


def _retention(log_g, rq16, rk32, rv16, gates, s0, batch, rows_per_batch, row0, c, shared_s0):
    n_c = rows_per_batch // c
    rb = row0 // c
    qkw, vw = HEADS * RET_DK, HEADS * RET_DV
    row = lambda b, ci: (rb + b * n_c + ci, 0)
    sidx = (lambda b, ci: (0, 0, 0, 0)) if shared_s0 else (lambda b, ci: (b, 0, 0, 0))
    return pl.pallas_call(
        functools.partial(_ret_kernel, c=c, n_c=n_c),
        grid=(batch, n_c),
        in_specs=[
            pl.BlockSpec(memory_space=pltpu.SMEM),
            pl.BlockSpec((c, qkw), row),
            pl.BlockSpec((c, qkw), row),
            pl.BlockSpec((c, vw), row),
            pl.BlockSpec((c, vw), row),
            pl.BlockSpec((1, HEADS, RET_DK, RET_DV), sidx),
        ],
        out_specs=[
            pl.BlockSpec((c, vw), lambda b, ci: (b * n_c + ci, 0)),
            pl.BlockSpec((1, HEADS, RET_DK, RET_DV), lambda b, ci: (b, 0, 0, 0)),
        ],
        out_shape=[
            jax.ShapeDtypeStruct((batch * rows_per_batch, vw), BF16),
            jax.ShapeDtypeStruct((batch, HEADS, RET_DK, RET_DV), F32),
        ],
        scratch_shapes=[pltpu.VMEM((HEADS, RET_DK, RET_DV), F32), pltpu.VMEM((HEADS, c, c), F32)],
        compiler_params=_cparams("arbitrary", "arbitrary"),
        name="retention",
    )(log_g, rq16, rk32, rv16, gates, s0)


def _route(logits):
    gl = [logits[r:r + 1, :] for r in range(N_GROUPS)]
    gmax = functools.reduce(jnp.maximum, gl)
    g_idx = jnp.where(gl[0] == gmax, 0, jnp.where(gl[1] == gmax, 1, jnp.where(gl[2] == gmax, 2, 3)))
    g_top = 1.0 / functools.reduce(lambda a, b: a + b, [jnp.exp(x - gmax) for x in gl])
    e = []
    for r in range(EXPERTS_PER_GROUP):
        rows = [logits[N_GROUPS + g * EXPERTS_PER_GROUP + r:N_GROUPS + g * EXPERTS_PER_GROUP + r + 1, :]
                for g in range(N_GROUPS)]
        e.append(jnp.where(g_idx == 0, rows[0], jnp.where(g_idx == 1, rows[1], jnp.where(g_idx == 2, rows[2], rows[3]))))
    m1 = functools.reduce(jnp.maximum, e)
    i1 = jnp.where(e[0] == m1, 0, jnp.where(e[1] == m1, 1, jnp.where(e[2] == m1, 2, 3)))
    e2 = [jnp.where(i1 == r, -jnp.inf, e[r]) for r in range(EXPERTS_PER_GROUP)]
    m2 = functools.reduce(jnp.maximum, e2)
    i2 = jnp.where(e2[0] == m2, 0, jnp.where(e2[1] == m2, 1, jnp.where(e2[2] == m2, 2, 3)))
    p2 = jnp.exp(m2 - m1)
    w1 = g_top / (1.0 + p2)
    w2 = g_top * p2 / (1.0 + p2)
    lo = jnp.minimum(i1, i2)
    hi = jnp.maximum(i1, i2)
    w_lo = jnp.where(i1 < i2, w1, w2)
    w_hi = jnp.where(i1 < i2, w2, w1)
    pair = jnp.where(lo == 0, hi - 1, jnp.where(lo == 1, hi + 1, 5))
    return g_idx * N_PAIRS + pair, w_lo, w_hi


def _gate_merge_kernel(osp_ref, oss_ref, orp_ref, ors_ref, gsb_ref, gret_ref, wsb_ref, wret_ref, m_ref, *, n_p):
    is_p = pl.program_id(1) < n_p
    o_sb = jnp.where(is_p, osp_ref[...], oss_ref[...])
    o_ret = jnp.where(is_p, orp_ref[...], ors_ref[...])
    merged = (jax.nn.sigmoid(gsb_ref[...]) * _dot(o_sb, wsb_ref[...])
              + jax.nn.sigmoid(gret_ref[...]) * _dot(o_ret, wret_ref[...]))
    m_ref[...] = merged.astype(m_ref.dtype)


def _gate_merge(osp, oss, orp, ors, gates, wsb, wret, tm, tn):
    d = wsb.shape[1]
    tn = min(tn, d)
    n_p, n_s = osp.shape[0] // tm, oss.shape[0] // tm
    n_j = d // tn
    vw = HEADS * RET_DV
    pidx = lambda j, i: (jnp.minimum(i, n_p - 1), 0)
    sidx = lambda j, i: (jnp.clip(i - n_p, 0, n_s - 1), 0)
    return pl.pallas_call(
        functools.partial(_gate_merge_kernel, n_p=n_p),
        grid=(n_j, n_p + n_s),
        in_specs=[
            pl.BlockSpec((tm, HEADS * SB_DH), pidx),
            pl.BlockSpec((tm, HEADS * SB_DH), sidx),
            pl.BlockSpec((tm, vw), pidx),
            pl.BlockSpec((tm, vw), sidx),
            pl.BlockSpec((tm, tn), lambda j, i: (i, vw // tn + j)),
            pl.BlockSpec((tm, tn), lambda j, i: (i, (vw + d) // tn + j)),
            pl.BlockSpec((wsb.shape[0], tn), lambda j, i: (0, j)),
            pl.BlockSpec((wret.shape[0], tn), lambda j, i: (0, j)),
        ],
        out_specs=pl.BlockSpec((tm, tn), lambda j, i: (i, j)),
        out_shape=jax.ShapeDtypeStruct(((n_p + n_s) * tm, d), BF16),
        compiler_params=_cparams("arbitrary", "arbitrary"),
        name="gate_merge",
    )(osp, oss, orp, ors, gates, gates, wsb, wret)


def _out_kernel(m_ref, xp_ref, xs_ref, wout_ref, n2g_ref, wrc_ref, br_ref, h_ref, route_ref, *, n_p):
    x = jnp.where(pl.program_id(0) < n_p, xp_ref[...], xs_ref[...])
    d = x.shape[1]
    h = x + _dot(m_ref[...], wout_ref[...])
    h_ref[:, :d] = h
    u2 = _rms(h, n2g_ref[...])
    u_hi = u2.astype(BF16)
    u_lo = (u2 - u_hi.astype(F32)).astype(BF16)
    hi_both = _dot(u_hi, wrc_ref[...])
    logits = (hi_both[:, :128] + (hi_both[:, 128:] + _dot(u_lo, wrc_ref[:, :128]))) + br_ref[...]
    cls, w_lo, w_hi = _route(logits.T)
    route_ref[...] = jnp.zeros_like(route_ref)
    route_ref[0:1, :] = cls.astype(F32)
    w_rows = jnp.concatenate([w_lo, w_hi, jnp.zeros((126, w_lo.shape[1]), F32)], axis=0)
    h_ref[:, d:] = w_rows.T


def _out_proj(m, xp, xs, wout, n2g, wr_cat, br, tm):
    d = xp.shape[1]
    n_p, n_s = xp.shape[0] // tm, xs.shape[0] // tm
    n = n_p + n_s
    const = lambda i: (0, 0)
    return pl.pallas_call(
        functools.partial(_out_kernel, n_p=n_p),
        grid=(n,),
        in_specs=[
            pl.BlockSpec((tm, d), lambda i: (i, 0)),
            pl.BlockSpec((tm, d), lambda i: (jnp.minimum(i, n_p - 1), 0)),
            pl.BlockSpec((tm, d), lambda i: (jnp.clip(i - n_p, 0, n_s - 1), 0)),
            pl.BlockSpec(wout.shape, const, pipeline_mode=pl.Buffered(1)),
            pl.BlockSpec((1, d), const),
            pl.BlockSpec(wr_cat.shape, const),
            pl.BlockSpec(br.shape, const),
        ],
        out_specs=[
            pl.BlockSpec((tm, d + 128), lambda i: (i, 0)),
            pl.BlockSpec((8, tm), lambda i: (0, i)),
        ],
        out_shape=[
            jax.ShapeDtypeStruct((n * tm, d + 128), F32),
            jax.ShapeDtypeStruct((8, n * tm), F32),
        ],
        compiler_params=_cparams("arbitrary"),
        name="out_proj",
    )(m, xp, xs, wout, n2g, wr_cat, br)


def _moe_kernel(elo_ref, ehi_ref, nvalid_ref, nprompt_ref, src_ref, nxt_ref, h_hbm, n2g_ref, nfg_ref,
                wg_lo, wu_lo, wd_lo, wg_hi, wu_hi, wd_hi, yp_hbm, ys_hbm, hbuf0, hbuf1, ybuf0, ybuf1, trash,
                sem_in, sem_out, *, tm, n_p):
    t = pl.program_id(0)
    n_valid = nvalid_ref[t]
    n_prompt = nprompt_ref[t]
    live = n_valid > 0
    prev_live = (t > 0) & (nvalid_ref[jnp.maximum(t - 1, 0)] > 0)
    hbuf, ybuf = (hbuf0, hbuf1), (ybuf0, ybuf1)
    d = ybuf0.shape[-1]

    def start_gather(idx_ref, s):
        for r in range(tm):
            pltpu.make_async_copy(h_hbm.at[pl.ds(idx_ref[0, 0, r], 1), :], hbuf[s].at[pl.ds(r, 1), :],
                                  sem_in.at[s]).start()

    def wait_gather(s):
        pltpu.make_async_copy(h_hbm.at[pl.ds(0, tm), :], hbuf[s], sem_in.at[s]).wait()

    def wait_scatter(s):
        pltpu.make_async_copy(ybuf[s], yp_hbm.at[pl.ds(0, tm), :], sem_out.at[s]).wait()

    @pl.when(t == 0)
    def _():
        start_gather(src_ref, 0)

    def tile(slot):
        wait_gather(slot)
        start_gather(nxt_ref, 1 - slot)
        hrows = hbuf[slot][:, :d]
        wts = hbuf[slot][:, d:]
        u = _rms(hrows, n2g_ref[...]).astype(BF16)
        y = jnp.zeros_like(hrows)
        for col, (wg, wu, wd) in enumerate(((wg_lo, wu_lo, wd_lo), (wg_hi, wu_hi, wd_hi))):
            gate = _dot(u, wg[0])
            hid = (gate * jax.nn.sigmoid(gate)) * _dot(u, wu[0])
            y = y + wts[:, col:col + 1] * _dot(hid.astype(BF16), wd[0])
        ybuf[slot][...] = _rms(hrows + y, nfg_ref[...])

        @pl.when(prev_live)
        def _():
            wait_scatter(1 - slot)

        def to_prompt(r):
            pltpu.make_async_copy(ybuf[slot].at[pl.ds(r, 1), :], yp_hbm.at[pl.ds(src_ref[0, 0, r], 1), :],
                                  sem_out.at[slot]).start()

        def to_sample(r):
            pltpu.make_async_copy(ybuf[slot].at[pl.ds(r, 1), :], ys_hbm.at[pl.ds(src_ref[0, 0, r] - n_p, 1), :],
                                  sem_out.at[slot]).start()

        def to_trash(r):
            pltpu.make_async_copy(ybuf[slot].at[pl.ds(r, 1), :], trash.at[pl.ds(r, 1), :],
                                  sem_out.at[slot]).start()

        def scatter_chunk(c, carry):
            r0 = c * SCATTER_CHUNK
            r1 = r0 + SCATTER_CHUNK
            all_prompt = r1 <= n_prompt
            all_sample = (r0 >= n_prompt) & (r1 <= n_valid)
            all_pad = r0 >= n_valid

            @pl.when(all_prompt)
            def _():
                for k in range(SCATTER_CHUNK):
                    to_prompt(r0 + k)

            @pl.when(all_sample)
            def _():
                for k in range(SCATTER_CHUNK):
                    to_sample(r0 + k)

            @pl.when(all_pad)
            def _():
                for k in range(SCATTER_CHUNK):
                    to_trash(r0 + k)

            @pl.when(jnp.logical_not(all_prompt | all_sample | all_pad))
            def _():
                for k in range(SCATTER_CHUNK):
                    r = r0 + k
                    pl.when(r < n_prompt)(functools.partial(to_prompt, r))
                    pl.when((r >= n_prompt) & (r < n_valid))(functools.partial(to_sample, r))
                    pl.when(r >= n_valid)(functools.partial(to_trash, r))

            return carry

        lax.fori_loop(0, tm // SCATTER_CHUNK, scatter_chunk, 0)

        @pl.when(nvalid_ref[t + 1] == 0)
        def _():
            wait_scatter(slot)

    for slot in (0, 1):
        parity = t % 2 == slot
        pl.when(prev_live & jnp.logical_not(live) & parity)(functools.partial(wait_gather, slot))
        pl.when(live & parity)(functools.partial(tile, slot))


def _moe(h_ext, elo, ehi, nvalid, nprompt, src, n2g, nfg, wg, wu, wd, tm, n_p):
    n, dx = h_ext.shape
    d = wg.shape[1]
    f = wg.shape[-1]
    n_t = src.shape[0]
    lo = lambda t, elo, ehi, nv, npr: (elo[t], 0, 0)
    hi = lambda t, elo, ehi, nv, npr: (ehi[t], 0, 0)
    const = lambda t, *_: (0, 0)
    grid_spec = pltpu.PrefetchScalarGridSpec(
        num_scalar_prefetch=4,
        grid=(n_t,),
        in_specs=[
            pl.BlockSpec((1, 1, tm), lambda t, *_: (t, 0, 0), memory_space=pltpu.SMEM),
            pl.BlockSpec((1, 1, tm), lambda t, *_: (jnp.minimum(t + 1, n_t - 1), 0, 0), memory_space=pltpu.SMEM),
            pl.BlockSpec(memory_space=pl.ANY),
            pl.BlockSpec((1, d), const),
            pl.BlockSpec((1, d), const),
            pl.BlockSpec((1, d, f), lo),
            pl.BlockSpec((1, d, f), lo),
            pl.BlockSpec((1, f, d), lo),
            pl.BlockSpec((1, d, f), hi),
            pl.BlockSpec((1, d, f), hi),
            pl.BlockSpec((1, f, d), hi),
        ],
        out_specs=[pl.BlockSpec(memory_space=pl.ANY), pl.BlockSpec(memory_space=pl.ANY)],
        scratch_shapes=[
            pltpu.VMEM((tm, dx), F32),
            pltpu.VMEM((tm, dx), F32),
            pltpu.VMEM((tm, d), F32),
            pltpu.VMEM((tm, d), F32),
            pltpu.VMEM((tm, d), F32),
            pltpu.SemaphoreType.DMA((2,)),
            pltpu.SemaphoreType.DMA((2,)),
        ],
    )
    return pl.pallas_call(
        functools.partial(_moe_kernel, tm=tm, n_p=n_p),
        grid_spec=grid_spec,
        out_shape=[jax.ShapeDtypeStruct((n_p, d), F32), jax.ShapeDtypeStruct((n - n_p, d), F32)],
        compiler_params=_cparams("arbitrary"),
        name="moe",
    )(elo, ehi, nvalid, nprompt, src, src, h_ext, n2g, nfg, wg, wu, wd, wg, wu, wd)


def _sort_by_class(route, tm, n_p):
    n = route.shape[1]
    n_t = n // tm + N_CLASSES + 1
    cls = route[0].astype(jnp.int32)
    onehot = (cls[:, None] == jnp.arange(N_CLASSES, dtype=jnp.int32)[None, :]).astype(jnp.int32)
    rank = jnp.sum((jnp.cumsum(onehot, axis=0) - onehot) * onehot, axis=1)
    tiles_per = (jnp.sum(onehot, axis=0) + tm - 1) // tm
    tile_end = jnp.cumsum(tiles_per)
    nused = tile_end[-1]
    pos = (tile_end - tiles_per)[cls] * tm + rank
    tile_ids = jnp.minimum(jnp.arange(n_t, dtype=jnp.int32), nused - 1)
    tile_cls = jnp.minimum(jnp.sum((tile_ids[:, None] >= tile_end[None, :]).astype(jnp.int32), axis=1), N_CLASSES - 1)
    pair_lo = jnp.array([0, 0, 0, 1, 1, 2], jnp.int32)
    pair_hi = jnp.array([1, 2, 3, 2, 3, 3], jnp.int32)
    elo = (tile_cls // N_PAIRS) * EXPERTS_PER_GROUP + pair_lo[tile_cls % N_PAIRS]
    ehi = (tile_cls // N_PAIRS) * EXPERTS_PER_GROUP + pair_hi[tile_cls % N_PAIRS]
    p = n_t * tm
    src = jnp.full((p,), -1, jnp.int32).at[pos].set(jnp.arange(n, dtype=jnp.int32))
    nvalid = jnp.sum((src >= 0).reshape(n_t, tm), axis=1).astype(jnp.int32)
    nprompt = jnp.sum(((src >= 0) & (src < n_p)).reshape(n_t, tm), axis=1).astype(jnp.int32)
    src = jnp.maximum(src, 0)
    return elo, ehi, nvalid, nprompt, src.reshape(n_t, 1, tm)


def kernel(x_prompt, x_sample, cache_sb_k, cache_sb_v, state_ret, meta, norm1_g, w_in, w_sb_o, w_ret_o,
           w_out, norm2_g, w_grp, b_grp, w_exp, b_exp, w_gate, w_up, w_down, normf_g):
    batch, seq, d = x_prompt.shape
    dec_b, dec_n, _ = x_sample.shape
    depth, _, past = cache_sb_k.shape[:3]
    assert depth == 1
    n_p, n_s = batch * seq, dec_b * dec_n
    sbw, qkw, vw = HEADS * SB_DH, HEADS * RET_DK, HEADS * RET_DV
    tm = min(ROW_TILE, n_s)
    tm_norm = min(NORM_TILE, n_s)
    assert tm % tm_norm == 0 and n_p % tm == 0 and n_s % tm == 0

    xp = x_prompt.reshape(n_p, d)
    xs = x_sample.reshape(n_s, d)
    u, sb_q16 = _norm_and_project(xp, xs, meta, norm1_g, w_in[0], sbw, tm_norm, tm // tm_norm)
    meta_row0 = n_p + n_s

    assert seq % tm == 0 and tm % dec_n == 0
    pos = jnp.concatenate([
        jnp.arange(seq, dtype=F32),
        jnp.tile(past + jnp.arange(dec_n, dtype=F32), tm // dec_n),
        jnp.arange(tm, dtype=F32) - N_META,
    ])
    inv_freq = 1.0 / (ROPE_BASE ** (jnp.arange(0, RET_DK, 2, dtype=F32) / RET_DK))
    ang = pos[:, None] * inv_freq[None, :]
    cos_t = jnp.concatenate([jnp.cos(ang), jnp.cos(ang)], axis=1)
    sin_t = jnp.concatenate([-jnp.sin(ang), jnp.sin(ang)], axis=1)
    seq_tiles, n_pt, n_st = seq // tm, n_p // tm, n_s // tm

    def table_block(i):
        return jnp.where(i < n_pt, i % seq_tiles, jnp.where(i < n_pt + n_st, seq_tiles, seq_tiles + 1))

    proj = functools.partial(_project, u, w_in[0], tm=tm, tn=COL_TILE)
    c = sbw
    sb_k16, k_prompt, k_sample = _project_kv(u, w_in[0], c, tm, batch, seq, n_s); c += sbw
    sb_v16, v_prompt, v_sample = _project_kv(u, w_in[0], c, tm, batch, seq, n_s); c += sbw
    rot = (cos_t, sin_t, table_block)
    r_q16 = proj(c, qkw, "rot", rot=rot, out_dtype=BF16); c += qkw
    r_k32 = proj(c, qkw, "rot", rot=rot, scale=RET_DK ** -0.5, out_dtype=F32); c += qkw
    n_exp, _, d_exp = w_gate.shape[1:]
    r_v16, wu16 = proj(c, vw, "bf16", riders=(w_up.reshape(n_exp * d, d_exp),)); c += vw
    gates, wg16, wd16, wsb16, wret16, wout16 = proj(
        c, vw + 2 * d, "f32",
        riders=(w_gate.reshape(n_exp * d, d_exp), w_down.reshape(n_exp * d_exp, d),
                w_sb_o[0], w_ret_o[0], w_out[0]))
    wg16 = wg16.reshape(n_exp, d, d_exp)
    wu16 = wu16.reshape(n_exp, d, d_exp)
    wd16 = wd16.reshape(n_exp, d_exp, d)

    o_sb_p = _sb_prompt(sb_q16, sb_k16, sb_v16, batch, seq, meta_row0, min(SB_TILE, seq))
    o_sb_s = _sb_sample(sb_q16, sb_k16, sb_v16, cache_sb_k, cache_sb_v, dec_n, n_p, SB_TILE)

    log_g = jnp.log1p(-jnp.power(2.0, -5.0 - jnp.arange(HEADS, dtype=F32)))
    ret = functools.partial(_retention, log_g, r_q16, r_k32, r_v16, gates)
    zero_state = jnp.zeros((1, HEADS, RET_DK, RET_DV), F32)
    _, s_meta = ret(zero_state, 1, N_META, meta_row0, N_META, True)
    o_ret_p, s_p = ret(s_meta, batch, seq, 0, min(RET_CHUNK, seq), True)
    o_ret_s, s_s = ret(state_ret[0], dec_b, dec_n, n_p, dec_n, False)

    n_route = N_GROUPS + N_GROUPS * EXPERTS_PER_GROUP
    w_router = jnp.pad(jnp.concatenate([w_grp[0], w_exp[0]], axis=1), ((0, 0), (0, 128 - n_route)))
    b_router = jnp.pad(jnp.concatenate([b_grp[0], b_exp[0]]), (0, 128 - n_route)).reshape(1, 128)
    tmm = min(MERGE_TILE, n_s)
    merged = _gate_merge(o_sb_p, o_sb_s, o_ret_p, o_ret_s, gates,
                         wsb16, wret16, tmm, COL_TILE)
    wr_hi = w_router.astype(BF16)
    wr_lo = (w_router - wr_hi.astype(F32)).astype(BF16)
    wr_cat = jnp.concatenate([wr_hi, wr_lo], axis=1)
    h, route = _out_proj(merged, xp, xs, wout16, norm2_g, wr_cat, b_router, tmm)

    tmo = min(MOE_TILE, n_s)
    elo, ehi, nvalid, nprompt, src = _sort_by_class(route, tmo, n_p)
    y_p, y_s = _moe(h, elo, ehi, nvalid, nprompt, src, norm2_g, normf_g.reshape(1, d),
                    wg16, wu16, wd16, tmo, n_p)
    y_prompt = y_p.reshape(batch, seq, d)
    y_sample = y_s.reshape(dec_b, dec_n, d)

    pshape = (1, batch, N_META + seq, HEADS, SB_DH)
    sshape = (1, dec_b, dec_n, HEADS, SB_DH)
    return (y_prompt, y_sample, k_prompt.reshape(pshape), v_prompt.reshape(pshape), s_p[None],
            k_sample.reshape(sshape), v_sample.reshape(sshape), s_s[None])
```

```python
import functools

import jax
import jax.numpy as jnp
from jax import lax
from jax.experimental import pallas as pl
from jax.experimental.pallas import tpu as pltpu

F32 = jnp.float32
BF16 = jnp.bfloat16

N_META = 16
HEADS = 8
SB_DH = 128
RET_DK = 128
RET_DV = 256
N_GROUPS = 4
EXPERTS_PER_GROUP = 4
N_PAIRS = 6
N_CLASSES = N_GROUPS * N_PAIRS
ROPE_BASE = 10000.0
EPS = 1e-6
SB_LOG_CUTOFF = -88.0

ROW_TILE = 1024
NORM_TILE = 512
COL_TILE = 1024
MERGE_TILE = 512
MOE_TILE = 256
SCATTER_CHUNK = 32
SB_TILE = 128
RET_CHUNK = 256
VMEM_LIMIT = 56 * 1024 * 1024


def _cparams(*sem):
    return pltpu.CompilerParams(dimension_semantics=sem, vmem_limit_bytes=VMEM_LIMIT)


def _rms(x, g):
    return (x * lax.rsqrt(jnp.mean(x * x, axis=-1, keepdims=True) + EPS)) * g


def _dot(a, b):
    return jnp.dot(a, b, preferred_element_type=F32)


def _dot_nt(a, b):
    return lax.dot_general(a, b, (((1,), (1,)), ((), ())), preferred_element_type=F32)


def _dot_tn(a, b):
    return lax.dot_general(a, b, (((0,), (0,)), ((), ())), preferred_element_type=F32)


def _norm_proj_kernel(xp_ref, xs_ref, meta_ref, g_ref, w_ref, u_ref, o_ref, w16, *, n_p, n_s):
    i = pl.program_id(0)

    @pl.when(i == 0)
    def _():
        w16[...] = w_ref[...].astype(BF16)

    @pl.when(i < n_p)
    def _():
        u_ref[...] = _rms(xp_ref[...], g_ref[...]).astype(BF16)

    @pl.when((i >= n_p) & (i < n_p + n_s))
    def _():
        u_ref[...] = _rms(xs_ref[...], g_ref[...]).astype(BF16)

    @pl.when(i >= n_p + n_s)
    def _():
        u_ref[...] = jnp.zeros_like(u_ref)

    @pl.when(i == n_p + n_s)
    def _():
        u_ref[0:N_META, :] = _rms(meta_ref[...], g_ref[...]).astype(BF16)

    o_ref[...] = _dot(u_ref[...], w16[...]).astype(BF16)


def _norm_and_project(xp, xs, meta, g, w, width, tm, tail_tiles):
    d = xp.shape[1]
    n_p, n_s = xp.shape[0] // tm, xs.shape[0] // tm
    m = (n_p + n_s + tail_tiles) * tm
    return pl.pallas_call(
        functools.partial(_norm_proj_kernel, n_p=n_p, n_s=n_s),
        grid=(n_p + n_s + tail_tiles,),
        in_specs=[
            pl.BlockSpec((tm, d), lambda i: (jnp.minimum(i, n_p - 1), 0)),
            pl.BlockSpec((tm, d), lambda i: (jnp.clip(i - n_p, 0, n_s - 1), 0)),
            pl.BlockSpec((N_META, d), lambda i: (0, 0)),
            pl.BlockSpec((1, d), lambda i: (0, 0)),
            pl.BlockSpec((d, width), lambda i: (0, 0), pipeline_mode=pl.Buffered(1)),
        ],
        out_specs=[pl.BlockSpec((tm, d), lambda i: (i, 0)), pl.BlockSpec((tm, width), lambda i: (i, 0))],
        out_shape=[jax.ShapeDtypeStruct((m, d), BF16), jax.ShapeDtypeStruct((m, width), BF16)],
        scratch_shapes=[pltpu.VMEM((d, width), BF16)],
        compiler_params=_cparams("arbitrary"),
        name="norm_proj",
    )(xp, xs, meta, g, w)


def _proj_kernel(u_ref, w_ref, *refs, mode, scale, rider_blocks):
    w16 = refs[-1]
    n_rider = len(rider_blocks)
    riders_in, refs = refs[:n_rider], refs[n_rider:]

    @pl.when(pl.program_id(1) == 0)
    def _():
        w16[...] = w_ref[...].astype(BF16)

    riders_out = refs[len(refs) - 1 - n_rider:len(refs) - 1]
    for src, dst in zip(riders_in, riders_out):
        dst[...] = src[...].astype(BF16)

    acc = _dot(u_ref[...], w16[...])
    if mode == "f32":
        refs[0][...] = acc
    elif mode == "bf16":
        refs[0][...] = acc.astype(BF16)
    else:
        cos_ref, sin_ref, o_ref = refs[:3]
        cos, sin = cos_ref[...], sin_ref[...]
        for hh in range(acc.shape[1] // RET_DK):
            x = acc[:, hh * RET_DK:(hh + 1) * RET_DK]
            r = x * cos + pltpu.roll(x, RET_DK // 2, 1) * sin
            if scale is not None:
                r = r * scale
            o_ref[:, hh * RET_DK:(hh + 1) * RET_DK] = r.astype(o_ref.dtype)


def _project(u, w, col0, width, mode, tm, tn, rot=None, scale=None, out_dtype=None, riders=()):
    m, d = u.shape
    tn = min(tn, width)
    n_j, n_i = width // tn, m // tm
    j0 = col0 // tn
    w_mode = {"pipeline_mode": pl.Buffered(1)} if tn > COL_TILE else {}
    in_specs = [
        pl.BlockSpec((tm, d), lambda j, i: (i, 0)),
        pl.BlockSpec((d, tn), lambda j, i: (0, j0 + j), **w_mode),
    ]
    args = [u, w]
    rider_blocks, rider_specs = [], []
    for a in riders:
        nb = max(b for b in (1, 2, 4, 8, 16, 32, 64, 128) if b <= n_j * n_i and a.shape[0] % (16 * b) == 0)
        spec = pl.BlockSpec((a.shape[0] // nb, a.shape[1]), lambda j, i, nb=nb: (jnp.minimum(j * n_i + i, nb - 1), 0))
        rider_blocks.append(nb)
        rider_specs.append(spec)
    in_specs += rider_specs
    args += list(riders)
    out_specs = [pl.BlockSpec((tm, tn), lambda j, i: (i, j))]
    if mode == "rot":
        cos_t, sin_t, table_block = rot
        in_specs += [pl.BlockSpec((tm, RET_DK), lambda j, i: (table_block(i), 0))] * 2
        args += [cos_t, sin_t]
        out_shape = [jax.ShapeDtypeStruct((m, width), out_dtype)]
    else:
        out_shape = [jax.ShapeDtypeStruct((m, width), F32 if mode == "f32" else BF16)]
    out_specs += rider_specs
    out_shape += [jax.ShapeDtypeStruct(a.shape, BF16) for a in riders]
    res = pl.pallas_call(
        functools.partial(_proj_kernel, mode=mode, scale=scale, rider_blocks=tuple(rider_blocks)),
        grid=(n_j, n_i),
        in_specs=in_specs,
        out_specs=out_specs,
        out_shape=out_shape,
        scratch_shapes=[pltpu.VMEM((d, tn), BF16)],
        compiler_params=_cparams("arbitrary", "arbitrary"),
        name="proj_" + mode,
    )(*args)
    return res if riders else res[0]


def _proj_kv_kernel(u_ref, w_ref, o16_ref, outp_hbm, outs_hbm, w16, stage, sem,
                    *, tm, n_pt, n_st, tiles_per_batch, batch, rows_per_batch):
    i = pl.program_id(0)

    @pl.when(i == 0)
    def _():
        w16[...] = w_ref[...].astype(BF16)

    acc = _dot(u_ref[...], w16[...])
    o16_ref[...] = acc.astype(BF16)

    def tile_copy(dst):
        return pltpu.make_async_copy(stage, dst, sem)

    @pl.when(i > 0)
    def _():
        tile_copy(outp_hbm.at[pl.ds(0, tm)]).wait()

    stage[...] = acc.reshape(tm, HEADS, SB_DH)

    @pl.when(i < n_pt)
    def _():
        b = i // tiles_per_batch
        r0 = (i % tiles_per_batch) * tm
        tile_copy(outp_hbm.at[pl.ds(b * rows_per_batch + N_META + r0, tm)]).start()

    @pl.when((i >= n_pt) & (i < n_pt + n_st))
    def _():
        tile_copy(outs_hbm.at[pl.ds((i - n_pt) * tm, tm)]).start()

    @pl.when(i == n_pt + n_st)
    def _():
        copies = [pltpu.make_async_copy(stage.at[pl.ds(0, N_META)],
                                        outp_hbm.at[pl.ds(b * rows_per_batch, N_META)], sem)
                  for b in range(batch)]
        for c in copies:
            c.start()
        for c in copies:
            c.wait()


def _project_kv(u, w, col0, tm, batch, seq, n_s):
    m, d = u.shape
    width = HEADS * SB_DH
    n_pt, n_st = batch * seq // tm, n_s // tm
    assert seq % tm == 0 and m // tm == n_pt + n_st + 1
    j0 = col0 // width
    rows_per_batch = N_META + seq
    return pl.pallas_call(
        functools.partial(_proj_kv_kernel, tm=tm, n_pt=n_pt, n_st=n_st, tiles_per_batch=seq // tm,
                          batch=batch, rows_per_batch=rows_per_batch),
        grid=(m // tm,),
        in_specs=[
            pl.BlockSpec((tm, d), lambda i: (i, 0)),
            pl.BlockSpec((d, width), lambda i: (0, j0), pipeline_mode=pl.Buffered(1)),
        ],
        out_specs=[
            pl.BlockSpec((tm, width), lambda i: (i, 0)),
            pl.BlockSpec(memory_space=pl.ANY),
            pl.BlockSpec(memory_space=pl.ANY),
        ],
        out_shape=[
            jax.ShapeDtypeStruct((m, width), BF16),
            jax.ShapeDtypeStruct((batch * rows_per_batch, HEADS, SB_DH), F32),
            jax.ShapeDtypeStruct((n_s, HEADS, SB_DH), F32),
        ],
        scratch_shapes=[
            pltpu.VMEM((d, width), BF16),
            pltpu.VMEM((tm, HEADS, SB_DH), F32),
            pltpu.SemaphoreType.DMA(()),
        ],
        compiler_params=_cparams("arbitrary"),
        name="proj_kv",
    )(u, w)


def _upper_ones(n):
    r = lax.broadcasted_iota(jnp.int32, (n, n), 0)
    c = lax.broadcasted_iota(jnp.int32, (n, n), 1)
    return (r > c).astype(BF16)


def _sb_tiles(tiles, tri):
    n = range(len(tiles))
    q, k, v, carry, acc, valid = zip(*tiles)
    z = [_dot_nt(q[i], k[i]) * (SB_DH ** -0.5) for i in n]
    e = [jnp.exp(-jnp.abs(z[i])) for i in n]
    sp = [jnp.maximum(z[i], 0.0) + jnp.log(1.0 + e[i]) for i in n]
    lf = [-sp[i] if valid[i] is None else jnp.where(valid[i], -sp[i], 0.0) for i in n]
    hi = [lf[i].astype(BF16) for i in n]
    lo = [(lf[i] - hi[i].astype(F32)).astype(BF16) for i in n]
    later = [_dot(hi[i], tri) + _dot(lo[i], tri) for i in n]
    a = [jnp.exp((z[i] - sp[i]) + (later[i] + carry[i])) for i in n]
    a = [a[i] if valid[i] is None else jnp.where(valid[i], a[i], 0.0) for i in n]
    acc = [acc[i] + _dot(a[i].astype(BF16), v[i]) for i in n]
    carry = [carry[i] + jnp.sum(lf[i], axis=-1, keepdims=True) for i in n]
    return list(zip(carry, acc))


def _sb_tile(q, k, v, carry, acc, tri, valid=None):
    return _sb_tiles([(q, k, v, carry, acc, valid)], tri)[0]


def _sb_prompt_kernel(q_ref, k_ref, v_ref, mk_ref, mv_ref, o_ref, *, t, n_q, unroll):
    tri = _upper_ones(t)
    tri2 = _upper_ones(2 * t)
    tri_m = _upper_ones(N_META)
    col_minus_row = (lax.broadcasted_iota(jnp.int32, (t, 2 * t), 1)
                     - lax.broadcasted_iota(jnp.int32, (t, 2 * t), 0))

    def window(i):
        r0 = pl.multiple_of(i * t, t)
        w0 = pl.multiple_of(jnp.maximum(i - 1, 0) * t, t)
        valid = col_minus_row < (r0 - w0)
        return (q_ref[pl.ds(r0, t), :], k_ref[pl.ds(w0, 2 * t), :], v_ref[pl.ds(w0, 2 * t), :],
                jnp.zeros((t, 1), F32), jnp.zeros((t, SB_DH), F32), valid)

    def older_keys(i, q, carry, acc):
        def cond(s):
            return (s[0] >= 0) & (jnp.max(s[1]) > SB_LOG_CUTOFF)

        def body(s):
            j, carry, acc = s
            c0 = pl.multiple_of(j * t, t)
            carry, acc = _sb_tile(q, k_ref[pl.ds(c0, t), :], v_ref[pl.ds(c0, t), :], carry, acc, tri)
            return j - 1, carry, acc

        _, carry, acc = lax.while_loop(cond, body, (i - 2, carry, acc))
        return lax.cond(
            jnp.max(carry) > SB_LOG_CUTOFF,
            lambda: _sb_tile(q, mk_ref[...], mv_ref[...], carry, acc, tri_m)[1],
            lambda: acc,
        )

    def group(p, _):
        tiles = [window(p * unroll + u) for u in range(unroll)]
        done = _sb_tiles(tiles, tri2)
        for u, (carry, acc) in enumerate(done):
            i = p * unroll + u
            q = tiles[u][0]
            acc = lax.cond(
                jnp.max(carry) > SB_LOG_CUTOFF,
                functools.partial(older_keys, i, q, carry, acc),
                lambda acc=acc: acc,
            )
            o_ref[pl.ds(pl.multiple_of(i * t, t), t), :] = acc.astype(o_ref.dtype)
        return 0

    lax.fori_loop(0, n_q // unroll, group, 0)


def _sb_prompt(q16, k16, v16, batch, seq, meta_row0, t):
    n_q = seq // t
    assert n_q >= 2
    unroll = max(u for u in (1, 2, 4, 8) if n_q % u == 0)
    mb = meta_row0 // N_META
    blk = lambda b, h: (b, h)
    mblk = lambda b, h: (mb, h)
    return pl.pallas_call(
        functools.partial(_sb_prompt_kernel, t=t, n_q=n_q, unroll=unroll),
        grid=(batch, HEADS),
        in_specs=[
            pl.BlockSpec((seq, SB_DH), blk),
            pl.BlockSpec((seq, SB_DH), blk),
            pl.BlockSpec((seq, SB_DH), blk),
            pl.BlockSpec((N_META, SB_DH), mblk),
            pl.BlockSpec((N_META, SB_DH), mblk),
        ],
        out_specs=pl.BlockSpec((seq, SB_DH), blk),
        out_shape=jax.ShapeDtypeStruct((batch * seq, HEADS * SB_DH), BF16),
        compiler_params=_cparams("arbitrary", "arbitrary"),
        name="sb_prompt",
    )(q16, k16, v16, k16, v16)


def _sb_sample_kernel(q_ref, k_ref, v_ref, ckl_ref, cvl_ref, ck_hbm, cv_hbm, o_ref, kbuf, vbuf, sem,
                      *, n, t, n_past):
    b = pl.program_id(0)
    tri_n = _upper_ones(n)
    tri = _upper_ones(t)
    causal = lax.broadcasted_iota(jnp.int32, (n, n), 1) < lax.broadcasted_iota(jnp.int32, (n, n), 0)

    def older_keys(hd, q, carry, acc):
        def cond(s):
            return (s[0] >= 0) & (jnp.max(s[1]) > SB_LOG_CUTOFF)

        def body(s):
            j, carry, acc = s
            rows = pl.ds(pl.multiple_of(j * t, t), t)
            ck = pltpu.make_async_copy(ck_hbm.at[0, b, rows], kbuf, sem.at[0])
            cv = pltpu.make_async_copy(cv_hbm.at[0, b, rows], vbuf, sem.at[1])
            ck.start()
            cv.start()
            ck.wait()
            cv.wait()
            carry, acc = _sb_tile(q, kbuf[:, hd, :].astype(BF16), vbuf[:, hd, :].astype(BF16), carry, acc, tri)
            return j - 1, carry, acc

        return lax.while_loop(cond, body, (n_past - 2, carry, acc))[2]

    cols = [slice(hd * SB_DH, (hd + 1) * SB_DH) for hd in range(HEADS)]
    qs = [q_ref[:, c] for c in cols]
    new = _sb_tiles([(qs[hd], k_ref[:, cols[hd]], v_ref[:, cols[hd]], jnp.zeros((n, 1), F32),
                      jnp.zeros((n, SB_DH), F32), causal) for hd in range(HEADS)], tri_n)
    old = _sb_tiles([(qs[hd], ckl_ref[0, 0, :, hd, :].astype(BF16), cvl_ref[0, 0, :, hd, :].astype(BF16),
                      new[hd][0], new[hd][1], None) for hd in range(HEADS)], tri)
    for hd, (carry, acc) in enumerate(old):
        q = qs[hd]
        acc = lax.cond(
            jnp.max(carry) > SB_LOG_CUTOFF,
            functools.partial(older_keys, hd, q, carry, acc),
            lambda acc=acc: acc,
        )
        o_ref[:, hd * SB_DH:(hd + 1) * SB_DH] = acc.astype(o_ref.dtype)


def _sb_sample(q16, k16, v16, cache_k, cache_v, n, row0, t):
    _, dec_b, past = cache_k.shape[:3]
    t = min(t, past)
    n_past = past // t
    rb = row0 // n
    w = HEADS * SB_DH
    blk = lambda b: (rb + b, 0)
    last = lambda b: (0, b, n_past - 1, 0, 0)
    return pl.pallas_call(
        functools.partial(_sb_sample_kernel, n=n, t=t, n_past=n_past),
        grid=(dec_b,),
        in_specs=[
            pl.BlockSpec((n, w), blk),
            pl.BlockSpec((n, w), blk),
            pl.BlockSpec((n, w), blk),
            pl.BlockSpec((1, 1, t, HEADS, SB_DH), last),
            pl.BlockSpec((1, 1, t, HEADS, SB_DH), last),
            pl.BlockSpec(memory_space=pl.ANY),
            pl.BlockSpec(memory_space=pl.ANY),
        ],
        out_specs=pl.BlockSpec((n, w), lambda b: (b, 0)),
        out_shape=jax.ShapeDtypeStruct((dec_b * n, w), BF16),
        scratch_shapes=[
            pltpu.VMEM((t, HEADS, SB_DH), F32),
            pltpu.VMEM((t, HEADS, SB_DH), F32),
            pltpu.SemaphoreType.DMA((2,)),
        ],
        compiler_params=_cparams("arbitrary"),
        name="sb_sample",
    )(q16, k16, v16, cache_k, cache_v, cache_k, cache_v)


def _ret_kernel(lg_ref, q_ref, k_ref, v_ref, g_ref, s0_ref, o_ref, s_out_ref, s_scr, d_scr, *, c, n_c):
    ci = pl.program_id(1)

    @pl.when((pl.program_id(0) == 0) & (ci == 0))
    def _():
        rel = (lax.broadcasted_iota(jnp.int32, (c, c), 0) - lax.broadcasted_iota(jnp.int32, (c, c), 1)).astype(F32)
        for hd in range(HEADS):
            d_scr[hd] = jnp.where(rel >= 0, jnp.exp(jnp.maximum(rel, 0.0) * lg_ref[hd]), 0.0)

    @pl.when(ci == 0)
    def _():
        s_scr[...] = s0_ref[0]

    tcol = lax.broadcasted_iota(jnp.int32, (c, 1), 0).astype(F32)
    for hd in range(HEADS):
        lg = lg_ref[hd]
        qk = slice(hd * RET_DK, (hd + 1) * RET_DK)
        vv = slice(hd * RET_DV, (hd + 1) * RET_DV)
        q = q_ref[:, qk]
        kf = k_ref[:, qk]
        v = v_ref[:, vv]
        s = s_scr[hd]
        scores = _dot_nt(q, kf.astype(BF16)) * d_scr[hd]
        o = _dot(scores.astype(BF16), v) + _dot(q, s.astype(BF16)) * jnp.exp((tcol + 1.0) * lg)
        k_dec = (kf * jnp.exp((c - 1.0 - tcol) * lg)).astype(BF16)
        s_scr[hd] = jnp.exp(c * lg) * s + _dot_tn(k_dec, v)
        o = o * lax.rsqrt(jnp.mean(o * o, axis=-1, keepdims=True) + EPS)
        g = g_ref[:, vv]
        o_ref[:, vv] = (o * (g * jax.nn.sigmoid(g))).astype(o_ref.dtype)

    @pl.when(ci == n_c - 1)
    def _():
        s_out_ref[0] = s_scr[...]


def _retention(log_g, rq16, rk32, rv16, gates, s0, batch, rows_per_batch, row0, c, shared_s0):
    n_c = rows_per_batch // c
    rb = row0 // c
    qkw, vw = HEADS * RET_DK, HEADS * RET_DV
    row = lambda b, ci: (rb + b * n_c + ci, 0)
    sidx = (lambda b, ci: (0, 0, 0, 0)) if shared_s0 else (lambda b, ci: (b, 0, 0, 0))
    return pl.pallas_call(
        functools.partial(_ret_kernel, c=c, n_c=n_c),
        grid=(batch, n_c),
        in_specs=[
            pl.BlockSpec(memory_space=pltpu.SMEM),
            pl.BlockSpec((c, qkw), row),
            pl.BlockSpec((c, qkw), row),
            pl.BlockSpec((c, vw), row),
            pl.BlockSpec((c, vw), row),
            pl.BlockSpec((1, HEADS, RET_DK, RET_DV), sidx),
        ],
        out_specs=[
            pl.BlockSpec((c, vw), lambda b, ci: (b * n_c + ci, 0)),
            pl.BlockSpec((1, HEADS, RET_DK, RET_DV), lambda b, ci: (b, 0, 0, 0)),
        ],
        out_shape=[
            jax.ShapeDtypeStruct((batch * rows_per_batch, vw), BF16),
            jax.ShapeDtypeStruct((batch, HEADS, RET_DK, RET_DV), F32),
        ],
        scratch_shapes=[pltpu.VMEM((HEADS, RET_DK, RET_DV), F32), pltpu.VMEM((HEADS, c, c), F32)],
        compiler_params=_cparams("arbitrary", "arbitrary"),
        name="retention",
    )(log_g, rq16, rk32, rv16, gates, s0)


def _route(logits):
    gl = [logits[r:r + 1, :] for r in range(N_GROUPS)]
    gmax = functools.reduce(jnp.maximum, gl)
    g_idx = jnp.where(gl[0] == gmax, 0, jnp.where(gl[1] == gmax, 1, jnp.where(gl[2] == gmax, 2, 3)))
    g_top = 1.0 / functools.reduce(lambda a, b: a + b, [jnp.exp(x - gmax) for x in gl])
    e = []
    for r in range(EXPERTS_PER_GROUP):
        rows = [logits[N_GROUPS + g * EXPERTS_PER_GROUP + r:N_GROUPS + g * EXPERTS_PER_GROUP + r + 1, :]
                for g in range(N_GROUPS)]
        e.append(jnp.where(g_idx == 0, rows[0], jnp.where(g_idx == 1, rows[1], jnp.where(g_idx == 2, rows[2], rows[3]))))
    m1 = functools.reduce(jnp.maximum, e)
    i1 = jnp.where(e[0] == m1, 0, jnp.where(e[1] == m1, 1, jnp.where(e[2] == m1, 2, 3)))
    e2 = [jnp.where(i1 == r, -jnp.inf, e[r]) for r in range(EXPERTS_PER_GROUP)]
    m2 = functools.reduce(jnp.maximum, e2)
    i2 = jnp.where(e2[0] == m2, 0, jnp.where(e2[1] == m2, 1, jnp.where(e2[2] == m2, 2, 3)))
    p2 = jnp.exp(m2 - m1)
    w1 = g_top / (1.0 + p2)
    w2 = g_top * p2 / (1.0 + p2)
    lo = jnp.minimum(i1, i2)
    hi = jnp.maximum(i1, i2)
    w_lo = jnp.where(i1 < i2, w1, w2)
    w_hi = jnp.where(i1 < i2, w2, w1)
    pair = jnp.where(lo == 0, hi - 1, jnp.where(lo == 1, hi + 1, 5))
    return g_idx * N_PAIRS + pair, w_lo, w_hi


def _gate_merge_kernel(osp_ref, oss_ref, orp_ref, ors_ref, gsb_ref, gret_ref, wsb_ref, wret_ref, m_ref, *, n_p):
    is_p = pl.program_id(1) < n_p
    o_sb = jnp.where(is_p, osp_ref[...], oss_ref[...])
    o_ret = jnp.where(is_p, orp_ref[...], ors_ref[...])
    merged = (jax.nn.sigmoid(gsb_ref[...]) * _dot(o_sb, wsb_ref[...])
              + jax.nn.sigmoid(gret_ref[...]) * _dot(o_ret, wret_ref[...]))
    m_ref[...] = merged.astype(m_ref.dtype)


def _gate_merge(osp, oss, orp, ors, gates, wsb, wret, tm, tn):
    d = wsb.shape[1]
    tn = min(tn, d)
    n_p, n_s = osp.shape[0] // tm, oss.shape[0] // tm
    n_j = d // tn
    vw = HEADS * RET_DV
    pidx = lambda j, i: (jnp.minimum(i, n_p - 1), 0)
    sidx = lambda j, i: (jnp.clip(i - n_p, 0, n_s - 1), 0)
    return pl.pallas_call(
        functools.partial(_gate_merge_kernel, n_p=n_p),
        grid=(n_j, n_p + n_s),
        in_specs=[
            pl.BlockSpec((tm, HEADS * SB_DH), pidx),
            pl.BlockSpec((tm, HEADS * SB_DH), sidx),
            pl.BlockSpec((tm, vw), pidx),
            pl.BlockSpec((tm, vw), sidx),
            pl.BlockSpec((tm, tn), lambda j, i: (i, vw // tn + j)),
            pl.BlockSpec((tm, tn), lambda j, i: (i, (vw + d) // tn + j)),
            pl.BlockSpec((wsb.shape[0], tn), lambda j, i: (0, j)),
            pl.BlockSpec((wret.shape[0], tn), lambda j, i: (0, j)),
        ],
        out_specs=pl.BlockSpec((tm, tn), lambda j, i: (i, j)),
        out_shape=jax.ShapeDtypeStruct(((n_p + n_s) * tm, d), BF16),
        compiler_params=_cparams("arbitrary", "arbitrary"),
        name="gate_merge",
    )(osp, oss, orp, ors, gates, gates, wsb, wret)


def _out_kernel(m_ref, xp_ref, xs_ref, wout_ref, n2g_ref, wrc_ref, br_ref, h_ref, route_ref, *, n_p):
    x = jnp.where(pl.program_id(0) < n_p, xp_ref[...], xs_ref[...])
    d = x.shape[1]
    h = x + _dot(m_ref[...], wout_ref[...])
    h_ref[:, :d] = h
    u2 = _rms(h, n2g_ref[...])
    u_hi = u2.astype(BF16)
    u_lo = (u2 - u_hi.astype(F32)).astype(BF16)
    hi_both = _dot(u_hi, wrc_ref[...])
    logits = (hi_both[:, :128] + (hi_both[:, 128:] + _dot(u_lo, wrc_ref[:, :128]))) + br_ref[...]
    cls, w_lo, w_hi = _route(logits.T)
    route_ref[...] = jnp.zeros_like(route_ref)
    route_ref[0:1, :] = cls.astype(F32)
    w_rows = jnp.concatenate([w_lo, w_hi, jnp.zeros((126, w_lo.shape[1]), F32)], axis=0)
    h_ref[:, d:] = w_rows.T


def _out_proj(m, xp, xs, wout, n2g, wr_cat, br, tm):
    d = xp.shape[1]
    n_p, n_s = xp.shape[0] // tm, xs.shape[0] // tm
    n = n_p + n_s
    const = lambda i: (0, 0)
    return pl.pallas_call(
        functools.partial(_out_kernel, n_p=n_p),
        grid=(n,),
        in_specs=[
            pl.BlockSpec((tm, d), lambda i: (i, 0)),
            pl.BlockSpec((tm, d), lambda i: (jnp.minimum(i, n_p - 1), 0)),
            pl.BlockSpec((tm, d), lambda i: (jnp.clip(i - n_p, 0, n_s - 1), 0)),
            pl.BlockSpec(wout.shape, const, pipeline_mode=pl.Buffered(1)),
            pl.BlockSpec((1, d), const),
            pl.BlockSpec(wr_cat.shape, const),
            pl.BlockSpec(br.shape, const),
        ],
        out_specs=[
            pl.BlockSpec((tm, d + 128), lambda i: (i, 0)),
            pl.BlockSpec((8, tm), lambda i: (0, i)),
        ],
        out_shape=[
            jax.ShapeDtypeStruct((n * tm, d + 128), F32),
            jax.ShapeDtypeStruct((8, n * tm), F32),
        ],
        compiler_params=_cparams("arbitrary"),
        name="out_proj",
    )(m, xp, xs, wout, n2g, wr_cat, br)


def _moe_kernel(elo_ref, ehi_ref, nvalid_ref, nprompt_ref, src_ref, nxt_ref, h_hbm, n2g_ref, nfg_ref,
                wg_lo, wu_lo, wd_lo, wg_hi, wu_hi, wd_hi, yp_hbm, ys_hbm, hbuf0, hbuf1, ybuf0, ybuf1, trash,
                sem_in, sem_out, *, tm, n_p):
    t = pl.program_id(0)
    n_valid = nvalid_ref[t]
    n_prompt = nprompt_ref[t]
    live = n_valid > 0
    prev_live = (t > 0) & (nvalid_ref[jnp.maximum(t - 1, 0)] > 0)
    hbuf, ybuf = (hbuf0, hbuf1), (ybuf0, ybuf1)
    d = ybuf0.shape[-1]

    def start_gather(idx_ref, s):
        for r in range(tm):
            pltpu.make_async_copy(h_hbm.at[pl.ds(idx_ref[0, 0, r], 1), :], hbuf[s].at[pl.ds(r, 1), :],
                                  sem_in.at[s]).start()

    def wait_gather(s):
        pltpu.make_async_copy(h_hbm.at[pl.ds(0, tm), :], hbuf[s], sem_in.at[s]).wait()

    def wait_scatter(s):
        pltpu.make_async_copy(ybuf[s], yp_hbm.at[pl.ds(0, tm), :], sem_out.at[s]).wait()

    @pl.when(t == 0)
    def _():
        start_gather(src_ref, 0)

    def tile(slot):
        wait_gather(slot)
        start_gather(nxt_ref, 1 - slot)
        hrows = hbuf[slot][:, :d]
        wts = hbuf[slot][:, d:]
        u = _rms(hrows, n2g_ref[...]).astype(BF16)
        y = jnp.zeros_like(hrows)
        for col, (wg, wu, wd) in enumerate(((wg_lo, wu_lo, wd_lo), (wg_hi, wu_hi, wd_hi))):
            gate = _dot(u, wg[0])
            hid = (gate * jax.nn.sigmoid(gate)) * _dot(u, wu[0])
            y = y + wts[:, col:col + 1] * _dot(hid.astype(BF16), wd[0])
        ybuf[slot][...] = _rms(hrows + y, nfg_ref[...])

        @pl.when(prev_live)
        def _():
            wait_scatter(1 - slot)

        def to_prompt(r):
            pltpu.make_async_copy(ybuf[slot].at[pl.ds(r, 1), :], yp_hbm.at[pl.ds(src_ref[0, 0, r], 1), :],
                                  sem_out.at[slot]).start()

        def to_sample(r):
            pltpu.make_async_copy(ybuf[slot].at[pl.ds(r, 1), :], ys_hbm.at[pl.ds(src_ref[0, 0, r] - n_p, 1), :],
                                  sem_out.at[slot]).start()

        def to_trash(r):
            pltpu.make_async_copy(ybuf[slot].at[pl.ds(r, 1), :], trash.at[pl.ds(r, 1), :],
                                  sem_out.at[slot]).start()

        def scatter_chunk(c, carry):
            r0 = c * SCATTER_CHUNK
            r1 = r0 + SCATTER_CHUNK
            all_prompt = r1 <= n_prompt
            all_sample = (r0 >= n_prompt) & (r1 <= n_valid)
            all_pad = r0 >= n_valid

            @pl.when(all_prompt)
            def _():
                for k in range(SCATTER_CHUNK):
                    to_prompt(r0 + k)

            @pl.when(all_sample)
            def _():
                for k in range(SCATTER_CHUNK):
                    to_sample(r0 + k)

            @pl.when(all_pad)
            def _():
                for k in range(SCATTER_CHUNK):
                    to_trash(r0 + k)

            @pl.when(jnp.logical_not(all_prompt | all_sample | all_pad))
            def _():
                for k in range(SCATTER_CHUNK):
                    r = r0 + k
                    pl.when(r < n_prompt)(functools.partial(to_prompt, r))
                    pl.when((r >= n_prompt) & (r < n_valid))(functools.partial(to_sample, r))
                    pl.when(r >= n_valid)(functools.partial(to_trash, r))

            return carry

        lax.fori_loop(0, tm // SCATTER_CHUNK, scatter_chunk, 0)

        @pl.when(nvalid_ref[t + 1] == 0)
        def _():
            wait_scatter(slot)

    for slot in (0, 1):
        parity = t % 2 == slot
        pl.when(prev_live & jnp.logical_not(live) & parity)(functools.partial(wait_gather, slot))
        pl.when(live & parity)(functools.partial(tile, slot))


def _moe(h_ext, elo, ehi, nvalid, nprompt, src, n2g, nfg, wg, wu, wd, tm, n_p):
    n, dx = h_ext.shape
    d = wg.shape[1]
    f = wg.shape[-1]
    n_t = src.shape[0]
    lo = lambda t, elo, ehi, nv, npr: (elo[t], 0, 0)
    hi = lambda t, elo, ehi, nv, npr: (ehi[t], 0, 0)
    const = lambda t, *_: (0, 0)
    grid_spec = pltpu.PrefetchScalarGridSpec(
        num_scalar_prefetch=4,
        grid=(n_t,),
        in_specs=[
            pl.BlockSpec((1, 1, tm), lambda t, *_: (t, 0, 0), memory_space=pltpu.SMEM),
            pl.BlockSpec((1, 1, tm), lambda t, *_: (jnp.minimum(t + 1, n_t - 1), 0, 0), memory_space=pltpu.SMEM),
            pl.BlockSpec(memory_space=pl.ANY),
            pl.BlockSpec((1, d), const),
            pl.BlockSpec((1, d), const),
            pl.BlockSpec((1, d, f), lo),
            pl.BlockSpec((1, d, f), lo),
            pl.BlockSpec((1, f, d), lo),
            pl.BlockSpec((1, d, f), hi),
            pl.BlockSpec((1, d, f), hi),
            pl.BlockSpec((1, f, d), hi),
        ],
        out_specs=[pl.BlockSpec(memory_space=pl.ANY), pl.BlockSpec(memory_space=pl.ANY)],
        scratch_shapes=[
            pltpu.VMEM((tm, dx), F32),
            pltpu.VMEM((tm, dx), F32),
            pltpu.VMEM((tm, d), F32),
            pltpu.VMEM((tm, d), F32),
            pltpu.VMEM((tm, d), F32),
            pltpu.SemaphoreType.DMA((2,)),
            pltpu.SemaphoreType.DMA((2,)),
        ],
    )
    return pl.pallas_call(
        functools.partial(_moe_kernel, tm=tm, n_p=n_p),
        grid_spec=grid_spec,
        out_shape=[jax.ShapeDtypeStruct((n_p, d), F32), jax.ShapeDtypeStruct((n - n_p, d), F32)],
        compiler_params=_cparams("arbitrary"),
        name="moe",
    )(elo, ehi, nvalid, nprompt, src, src, h_ext, n2g, nfg, wg, wu, wd, wg, wu, wd)


def _sort_by_class(route, tm, n_p):
    n = route.shape[1]
    n_t = n // tm + N_CLASSES + 1
    cls = route[0].astype(jnp.int32)
    onehot = (cls[:, None] == jnp.arange(N_CLASSES, dtype=jnp.int32)[None, :]).astype(jnp.int32)
    rank = jnp.sum((jnp.cumsum(onehot, axis=0) - onehot) * onehot, axis=1)
    tiles_per = (jnp.sum(onehot, axis=0) + tm - 1) // tm
    tile_end = jnp.cumsum(tiles_per)
    nused = tile_end[-1]
    pos = (tile_end - tiles_per)[cls] * tm + rank
    tile_ids = jnp.minimum(jnp.arange(n_t, dtype=jnp.int32), nused - 1)
    tile_cls = jnp.minimum(jnp.sum((tile_ids[:, None] >= tile_end[None, :]).astype(jnp.int32), axis=1), N_CLASSES - 1)
    pair_lo = jnp.array([0, 0, 0, 1, 1, 2], jnp.int32)
    pair_hi = jnp.array([1, 2, 3, 2, 3, 3], jnp.int32)
    elo = (tile_cls // N_PAIRS) * EXPERTS_PER_GROUP + pair_lo[tile_cls % N_PAIRS]
    ehi = (tile_cls // N_PAIRS) * EXPERTS_PER_GROUP + pair_hi[tile_cls % N_PAIRS]
    p = n_t * tm
    src = jnp.full((p,), -1, jnp.int32).at[pos].set(jnp.arange(n, dtype=jnp.int32))
    nvalid = jnp.sum((src >= 0).reshape(n_t, tm), axis=1).astype(jnp.int32)
    nprompt = jnp.sum(((src >= 0) & (src < n_p)).reshape(n_t, tm), axis=1).astype(jnp.int32)
    src = jnp.maximum(src, 0)
    return elo, ehi, nvalid, nprompt, src.reshape(n_t, 1, tm)


def kernel(x_prompt, x_sample, cache_sb_k, cache_sb_v, state_ret, meta, norm1_g, w_in, w_sb_o, w_ret_o,
           w_out, norm2_g, w_grp, b_grp, w_exp, b_exp, w_gate, w_up, w_down, normf_g):
    batch, seq, d = x_prompt.shape
    dec_b, dec_n, _ = x_sample.shape
    depth, _, past = cache_sb_k.shape[:3]
    assert depth == 1
    n_p, n_s = batch * seq, dec_b * dec_n
    sbw, qkw, vw = HEADS * SB_DH, HEADS * RET_DK, HEADS * RET_DV
    tm = min(ROW_TILE, n_s)
    tm_norm = min(NORM_TILE, n_s)
    assert tm % tm_norm == 0 and n_p % tm == 0 and n_s % tm == 0

    xp = x_prompt.reshape(n_p, d)
    xs = x_sample.reshape(n_s, d)
    u, sb_q16 = _norm_and_project(xp, xs, meta, norm1_g, w_in[0], sbw, tm_norm, tm // tm_norm)
    meta_row0 = n_p + n_s

    assert seq % tm == 0 and tm % dec_n == 0
    pos = jnp.concatenate([
        jnp.arange(seq, dtype=F32),
        jnp.tile(past + jnp.arange(dec_n, dtype=F32), tm // dec_n),
        jnp.arange(tm, dtype=F32) - N_META,
    ])
    inv_freq = 1.0 / (ROPE_BASE ** (jnp.arange(0, RET_DK, 2, dtype=F32) / RET_DK))
    ang = pos[:, None] * inv_freq[None, :]
    cos_t = jnp.concatenate([jnp.cos(ang), jnp.cos(ang)], axis=1)
    sin_t = jnp.concatenate([-jnp.sin(ang), jnp.sin(ang)], axis=1)
    seq_tiles, n_pt, n_st = seq // tm, n_p // tm, n_s // tm

    def table_block(i):
        return jnp.where(i < n_pt, i % seq_tiles, jnp.where(i < n_pt + n_st, seq_tiles, seq_tiles + 1))

    proj = functools.partial(_project, u, w_in[0], tm=tm, tn=COL_TILE)
    c = sbw
    sb_k16, k_prompt, k_sample = _project_kv(u, w_in[0], c, tm, batch, seq, n_s); c += sbw
    sb_v16, v_prompt, v_sample = _project_kv(u, w_in[0], c, tm, batch, seq, n_s); c += sbw
    rot = (cos_t, sin_t, table_block)
    n_exp, _, d_exp = w_gate.shape[1:]
    r_q16, wd16 = proj(c, qkw, "rot", rot=rot, out_dtype=BF16,
                       riders=(w_down.reshape(n_exp * d_exp, d),)); c += qkw
    r_k32 = proj(c, qkw, "rot", rot=rot, scale=RET_DK ** -0.5, out_dtype=F32); c += qkw
    r_v16, wu16 = proj(c, vw, "bf16", riders=(w_up.reshape(n_exp * d, d_exp),)); c += vw
    gates, wg16, wsb16, wret16, wout16 = proj(
        c, vw + 2 * d, "f32",
        riders=(w_gate.reshape(n_exp * d, d_exp), w_sb_o[0], w_ret_o[0], w_out[0]))
    wg16 = wg16.reshape(n_exp, d, d_exp)
    wu16 = wu16.reshape(n_exp, d, d_exp)
    wd16 = wd16.reshape(n_exp, d_exp, d)

    o_sb_p = _sb_prompt(sb_q16, sb_k16, sb_v16, batch, seq, meta_row0, min(SB_TILE, seq))
    o_sb_s = _sb_sample(sb_q16, sb_k16, sb_v16, cache_sb_k, cache_sb_v, dec_n, n_p, SB_TILE)

    log_g = jnp.log1p(-jnp.power(2.0, -5.0 - jnp.arange(HEADS, dtype=F32)))
    ret = functools.partial(_retention, log_g, r_q16, r_k32, r_v16, gates)
    zero_state = jnp.zeros((1, HEADS, RET_DK, RET_DV), F32)
    _, s_meta = ret(zero_state, 1, N_META, meta_row0, N_META, True)
    o_ret_p, s_p = ret(s_meta, batch, seq, 0, min(RET_CHUNK, seq), True)
    o_ret_s, s_s = ret(state_ret[0], dec_b, dec_n, n_p, dec_n, False)

    n_route = N_GROUPS + N_GROUPS * EXPERTS_PER_GROUP
    w_router = jnp.pad(jnp.concatenate([w_grp[0], w_exp[0]], axis=1), ((0, 0), (0, 128 - n_route)))
    b_router = jnp.pad(jnp.concatenate([b_grp[0], b_exp[0]]), (0, 128 - n_route)).reshape(1, 128)
    tmm = min(MERGE_TILE, n_s)
    merged = _gate_merge(o_sb_p, o_sb_s, o_ret_p, o_ret_s, gates,
                         wsb16, wret16, tmm, COL_TILE)
    wr_hi = w_router.astype(BF16)
    wr_lo = (w_router - wr_hi.astype(F32)).astype(BF16)
    wr_cat = jnp.concatenate([wr_hi, wr_lo], axis=1)
    h, route = _out_proj(merged, xp, xs, wout16, norm2_g, wr_cat, b_router, tmm)

    tmo = min(MOE_TILE, n_s)
    elo, ehi, nvalid, nprompt, src = _sort_by_class(route, tmo, n_p)
    y_p, y_s = _moe(h, elo, ehi, nvalid, nprompt, src, norm2_g, normf_g.reshape(1, d),
                    wg16, wu16, wd16, tmo, n_p)
    y_prompt = y_p.reshape(batch, seq, d)
    y_sample = y_s.reshape(dec_b, dec_n, d)

    pshape = (1, batch, N_META + seq, HEADS, SB_DH)
    sshape = (1, dec_b, dec_n, HEADS, SB_DH)
    return (y_prompt, y_sample, k_prompt.reshape(pshape), v_prompt.reshape(pshape), s_p[None],
            k_sample.reshape(sshape), v_sample.reshape(sshape), s_s[None])
```
